```python
import math
import jax, jax.numpy as jnp
from jax import lax
import numpy as np

D_MODEL = 1024
BATCH = 4
SEQ = 8192
DEPTH = 2

CHUNK = 64
QBLOCK = 128
EPS = 1e-6
NEG_INF = -1e30

LRU_WIDTH = 512
LRU_HEADS = 8
LRU_HEAD_DIM = LRU_WIDTH // LRU_HEADS
CONV_WIDTH = 4
LRU_C = 8.0

MLA_HEADS = 8
MLA_Q_LORA = 384
MLA_KV_LORA = 256
MLA_NOPE = 64
MLA_ROPE = 32
MLA_V = 64
ROPE_BASE = 10000.0

FOX_HEADS = 8
FOX_HEAD_DIM = 64
FOX_WIDTH = FOX_HEADS * FOX_HEAD_DIM

N_BRANCH = 3
D_FF = ((8 * D_MODEL // 3 + 255) // 256) * 256
PLE_DIM = 256

SPLIT_SIZES = (
    LRU_WIDTH,
    LRU_WIDTH,
    MLA_Q_LORA,
    MLA_KV_LORA + MLA_ROPE,
    FOX_WIDTH,
    FOX_WIDTH,
    FOX_WIDTH,
    FOX_HEADS,
    N_BRANCH * D_MODEL,
)
D_IN = 2 * LRU_WIDTH + MLA_Q_LORA + MLA_KV_LORA + MLA_ROPE + 3 * FOX_WIDTH + FOX_HEADS + N_BRANCH * D_MODEL

kernel_name = "hybrid_gated_rglru_mla_fox_encoder"


def rmsnorm(x, g):
    xf = x.astype(jnp.float32)
    y = xf * lax.rsqrt(jnp.mean(xf * xf, axis=-1, keepdims=True) + EPS)
    return (y * g.astype(jnp.float32)).astype(x.dtype)


def split_columns(z):
    idx = []
    acc = 0
    for s in SPLIT_SIZES[:-1]:
        acc += s
        idx.append(acc)
    return jnp.split(z, idx, axis=-1)


def rope(x, cos, sin):
    half = x.shape[-1] // 2
    x1, x2 = x[..., :half], x[..., half:]
    c = cos[None, :, None, :].astype(x.dtype)
    s = sin[None, :, None, :].astype(x.dtype)
    return jnp.concatenate([x1 * c - x2 * s, x2 * c + x1 * s], axis=-1)


def block_attention(q, k, v, scale, unit, decay=None):
    B, S, H, Dk = q.shape
    nb = S // QBLOCK
    q_blocks = q.reshape(B, nb, QBLOCK, H, Dk).transpose(1, 0, 2, 3, 4)
    key_unit = jnp.arange(S) // unit
    decay_t = None if decay is None else decay.transpose(0, 2, 1)

    def one_block(args):
        ib, q_blk = args
        s = jnp.einsum('bqhd,bkhd->bhqk', q_blk, k, preferred_element_type=jnp.float32) * scale
        if decay_t is not None:
            dq = lax.dynamic_slice_in_dim(decay_t, ib * QBLOCK, QBLOCK, axis=2)
            s = s + dq[:, :, :, None] - decay_t[:, :, None, :]
        q_unit = (ib * QBLOCK + jnp.arange(QBLOCK)) // unit
        mask = q_unit[:, None] >= key_unit[None, :]
        s = jnp.where(mask[None, None], s, NEG_INF)
        pr = jax.nn.softmax(s, axis=-1)
        return jnp.einsum('bhqk,bkhd->bqhd', pr.astype(v.dtype), v)

    out = lax.map(one_block, (jnp.arange(nb), q_blocks))
    return out.transpose(1, 0, 2, 3, 4).reshape(B, S, H, v.shape[-1])


def _lru_combine(left, right):
    a1, b1 = left
    a2, b2 = right
    return a1 * a2, a2 * b1 + b2


def rglru_branch(u, u_gate, conv_w, conv_b, wa, ba, wx, bx, lam):
    B, S, W = u.shape
    up = jnp.pad(u, ((0, 0), (CONV_WIDTH - 1, 0), (0, 0)))
    xc = conv_b + up[:, 0:S] * conv_w[0]
    for kk in range(1, CONV_WIDTH):
        xc = xc + up[:, kk:kk + S] * conv_w[kk]
    xh = xc.reshape(B, S, LRU_HEADS, LRU_HEAD_DIM)
    r = jax.nn.sigmoid(jnp.einsum('bshi,hij->bshj', xh, wa).reshape(B, S, W) + ba)
    ig = jax.nn.sigmoid(jnp.einsum('bshi,hij->bshj', xh, wx).reshape(B, S, W) + bx)
    log_a = -LRU_C * r.astype(jnp.float32) * jax.nn.softplus(-lam.astype(jnp.float32))
    a = jnp.exp(log_a)
    b = jnp.sqrt(-jnp.expm1(2.0 * log_a)) * (ig * xc).astype(jnp.float32)
    _, h = lax.associative_scan(_lru_combine, (a, b), axis=1)
    return h.astype(u.dtype) * jax.nn.gelu(u_gate)


def mla_branch(c_q, ckv_rope, q_norm, wuq, kv_norm, wukv, cos, sin):
    B, S, _ = c_q.shape
    q = (rmsnorm(c_q, q_norm) @ wuq).reshape(B, S, MLA_HEADS, MLA_NOPE + MLA_ROPE)
    q_nope, q_rope = q[..., :MLA_NOPE], q[..., MLA_NOPE:]
    c_kv, k_rope = ckv_rope[..., :MLA_KV_LORA], ckv_rope[..., MLA_KV_LORA:]
    kv = (rmsnorm(c_kv, kv_norm) @ wukv).reshape(B, S, MLA_HEADS, MLA_NOPE + MLA_V)
    k_nope, v = kv[..., :MLA_NOPE], kv[..., MLA_NOPE:]
    q_rope = rope(q_rope, cos, sin)
    k_rope = rope(k_rope[:, :, None, :], cos, sin)
    q_full = jnp.concatenate([q_nope, q_rope], axis=-1)
    k_full = jnp.concatenate([k_nope, jnp.broadcast_to(k_rope, (B, S, MLA_HEADS, MLA_ROPE))], axis=-1)
    o = block_attention(q_full, k_full, v, (MLA_NOPE + MLA_ROPE) ** -0.5, CHUNK)
    return o.reshape(B, S, MLA_HEADS * MLA_V)


def fox_branch(fq, fk, fv, f_logit, bf):
    B, S, _ = fq.shape
    q = fq.reshape(B, S, FOX_HEADS, FOX_HEAD_DIM)
    k = fk.reshape(B, S, FOX_HEADS, FOX_HEAD_DIM)
    v = fv.reshape(B, S, FOX_HEADS, FOX_HEAD_DIM)
    log_f = jax.nn.log_sigmoid((f_logit + bf).astype(jnp.float32))
    cum = jnp.cumsum(log_f, axis=1)
    o = block_attention(q, k, v, FOX_HEAD_DIM ** -0.5, 1, decay=cum)
    return o.reshape(B, S, FOX_WIDTH)


def setup_inputs(seed: int = 0) -> dict:
    key = jax.random.key(seed)
    ks = jax.random.split(key, 32)

    def nrm(k, shape, scale):
        return jax.random.normal(k, shape, jnp.float32) * scale

    def gain(k, shape):
        return 1.0 + 0.05 * jax.random.normal(k, shape, jnp.float32)

    u = jax.random.uniform(ks[10], (DEPTH, LRU_WIDTH), jnp.float32, 0.9, 0.999)
    a = u ** (1.0 / LRU_C)
    lru_lambda = jnp.log(a) - jnp.log1p(-a)

    return {
        "x": nrm(ks[0], (BATCH, SEQ, D_MODEL), 1.0),
        "p": nrm(ks[1], (DEPTH, BATCH, SEQ, PLE_DIM), 1.0),
        "mix_norm": gain(ks[2], (DEPTH, D_MODEL)),
        "w_in": nrm(ks[3], (DEPTH, D_MODEL, D_IN), D_MODEL ** -0.5),
        "gate_b": nrm(ks[4], (DEPTH, N_BRANCH * D_MODEL), 0.1),
        "conv_w": nrm(ks[5], (DEPTH, CONV_WIDTH, LRU_WIDTH), CONV_WIDTH ** -0.5),
        "conv_b": nrm(ks[6], (DEPTH, LRU_WIDTH), 0.1),
        "lru_wa": nrm(ks[7], (DEPTH, LRU_HEADS, LRU_HEAD_DIM, LRU_HEAD_DIM), LRU_HEAD_DIM ** -0.5),
        "lru_ba": nrm(ks[8], (DEPTH, LRU_WIDTH), 0.1),
        "lru_wx": nrm(ks[9], (DEPTH, LRU_HEADS, LRU_HEAD_DIM, LRU_HEAD_DIM), LRU_HEAD_DIM ** -0.5),
        "lru_bx": nrm(ks[11], (DEPTH, LRU_WIDTH), 0.1),
        "lru_lambda": lru_lambda,
        "mla_q_norm": gain(ks[12], (DEPTH, MLA_Q_LORA)),
        "mla_wuq": nrm(ks[13], (DEPTH, MLA_Q_LORA, MLA_HEADS * (MLA_NOPE + MLA_ROPE)), MLA_Q_LORA ** -0.5),
        "mla_kv_norm": gain(ks[14], (DEPTH, MLA_KV_LORA)),
        "mla_wukv": nrm(ks[15], (DEPTH, MLA_KV_LORA, MLA_HEADS * (MLA_NOPE + MLA_V)), MLA_KV_LORA ** -0.5),
        "fox_bf": jax.random.uniform(ks[16], (DEPTH, FOX_HEADS), jnp.float32, 1.0, 5.0),
        "w_br_a": nrm(ks[17], (DEPTH, LRU_WIDTH, D_MODEL), LRU_WIDTH ** -0.5),
        "w_br_b": nrm(ks[18], (DEPTH, MLA_HEADS * MLA_V, D_MODEL), (MLA_HEADS * MLA_V) ** -0.5),
        "w_br_c": nrm(ks[19], (DEPTH, FOX_WIDTH, D_MODEL), FOX_WIDTH ** -0.5),
        "w_o": nrm(ks[20], (DEPTH, D_MODEL, D_MODEL), D_MODEL ** -0.5),
        "ffn_norm": gain(ks[21], (DEPTH, D_MODEL)),
        "w_gate_up": nrm(ks[22], (DEPTH, D_MODEL, 2 * D_FF), D_MODEL ** -0.5),
        "w_down": nrm(ks[23], (DEPTH, D_FF, D_MODEL), D_FF ** -0.5),
        "ple_norm": gain(ks[24], (DEPTH, D_MODEL)),
        "w_ple_gate": nrm(ks[25], (DEPTH, D_MODEL, D_MODEL), D_MODEL ** -0.5),
        "w_ple": nrm(ks[26], (DEPTH, PLE_DIM, D_MODEL), PLE_DIM ** -0.5),
        "final_norm": gain(ks[27], (D_MODEL,)),
    }


def reference(x, p, mix_norm, w_in, gate_b, conv_w, conv_b, lru_wa, lru_ba, lru_wx, lru_bx, lru_lambda,
              mla_q_norm, mla_wuq, mla_kv_norm, mla_wukv, fox_bf, w_br_a, w_br_b, w_br_c, w_o,
              ffn_norm, w_gate_up, w_down, ple_norm, w_ple_gate, w_ple, final_norm):
    B, S, D = x.shape
    pos = jnp.arange(S, dtype=jnp.float32)
    inv_freq = ROPE_BASE ** (-jnp.arange(0, MLA_ROPE, 2, dtype=jnp.float32) / MLA_ROPE)
    ang = pos[:, None] * inv_freq[None, :]
    cos, sin = jnp.cos(ang), jnp.sin(ang)

    for i in range(DEPTH):
        h = rmsnorm(x, mix_norm[i])
        z = h @ w_in[i]
        u_rnn, u_gelu, c_q, ckv_rope, fq, fk, fv, f_logit, gate_logit = split_columns(z)
        y_a = rglru_branch(u_rnn, u_gelu, conv_w[i], conv_b[i], lru_wa[i], lru_ba[i],
                           lru_wx[i], lru_bx[i], lru_lambda[i]) @ w_br_a[i]
        y_b = mla_branch(c_q, ckv_rope, mla_q_norm[i], mla_wuq[i], mla_kv_norm[i],
                         mla_wukv[i], cos, sin) @ w_br_b[i]
        y_c = fox_branch(fq, fk, fv, f_logit, fox_bf[i]) @ w_br_c[i]
        g = jax.nn.sigmoid(gate_logit + gate_b[i]).reshape(B, S, N_BRANCH, D)
        merged = g[:, :, 0] * y_a + g[:, :, 1] * y_b + g[:, :, 2] * y_c
        x = x + merged @ w_o[i]
        hf = rmsnorm(x, ffn_norm[i]) @ w_gate_up[i]
        x = x + (jax.nn.silu(hf[..., :D_FF]) * hf[..., D_FF:]) @ w_down[i]
        pg = jax.nn.sigmoid(rmsnorm(x, ple_norm[i]) @ w_ple_gate[i])
        x = x + pg * (p[i] @ w_ple[i])
    return rmsnorm(x, final_norm)
```

```python
import functools
import math

import jax
import jax.numpy as jnp
from jax import lax
from jax.experimental import pallas as pl
from jax.experimental.pallas import tpu as pltpu

F32 = jnp.float32
BF16 = jnp.bfloat16

EPS = 1e-6
NEG_INF = -1e30
LOG2E = math.log2(math.e)

LRU_WIDTH = 512
LRU_HEADS = 8
CONV_WIDTH = 4
LRU_C = 8.0

MLA_HEADS = 8
MLA_Q_LORA = 384
MLA_KV_LORA = 256
MLA_NOPE = 64
MLA_ROPE = 32
MLA_V = 64
ROPE_BASE = 10000.0
MLA_CHUNK = 64

FOX_HEADS = 8
FOX_HEAD_DIM = 64
FOX_WIDTH = FOX_HEADS * FOX_HEAD_DIM

LANES = 128
HEAD_SLOT = LANES
ROPE_LANE0 = MLA_NOPE
FLOGIT_LANE0 = MLA_NOPE + MLA_ROPE

VMEM_LIMIT = 56 * 1024 * 1024


def _cparams(sem):
    return pltpu.CompilerParams(dimension_semantics=sem, vmem_limit_bytes=VMEM_LIMIT)


def _rms(xf, g):
    ms = jnp.mean(xf * xf, axis=-1, keepdims=True)
    return xf * lax.rsqrt(ms + EPS) * g


def _softplus(z):
    return jnp.maximum(z, 0.0) + jnp.log1p(jnp.exp(-jnp.abs(z)))


def _norm_matmul_kernel(x_ref, g_ref, w_ref, cs_ref, cb_ref, o_ref, h_scr, *, use_scale, use_bias, act):
    @pl.when(pl.program_id(1) == 0)
    def _():
        h_scr[...] = _rms(x_ref[...], g_ref[...]).astype(BF16)

    acc = jnp.dot(h_scr[...], w_ref[...], preferred_element_type=F32)
    if use_scale:
        acc = acc * cs_ref[...]
    if use_bias:
        acc = acc + cb_ref[...]
    if act == "sigmoid":
        acc = jax.nn.sigmoid(acc)
    o_ref[...] = acc.astype(o_ref.dtype)


def _norm_matmul(x, g, w, *, name, out_dtype, tn, tm=1024, col_scale=None, col_bias=None, act=None):
    T, D = x.shape
    N = w.shape[1]
    ones = jnp.ones((1, N), F32)
    cs = ones if col_scale is None else col_scale.reshape(1, N).astype(F32)
    cb = ones if col_bias is None else col_bias.reshape(1, N).astype(F32)
    kern = functools.partial(_norm_matmul_kernel, use_scale=col_scale is not None,
                             use_bias=col_bias is not None, act=act)
    return pl.pallas_call(
        kern,
        grid=(T // tm, N // tn),
        in_specs=[
            pl.BlockSpec((tm, D), lambda i, j: (i, 0)),
            pl.BlockSpec((1, D), lambda i, j: (0, 0)),
            pl.BlockSpec((D, tn), lambda i, j: (0, j)),
            pl.BlockSpec((1, tn), lambda i, j: (0, j)),
            pl.BlockSpec((1, tn), lambda i, j: (0, j)),
        ],
        out_specs=pl.BlockSpec((tm, tn), lambda i, j: (i, j)),
        out_shape=jax.ShapeDtypeStruct((T, N), out_dtype),
        scratch_shapes=[pltpu.VMEM((tm, D), BF16)],
        compiler_params=_cparams(("parallel", "arbitrary")),
        name=name,
    )(x, g.reshape(1, D), w, cs, cb)


def _lru_kernel(u_ref, cw_ref, cb_ref, wa_ref, ba_ref, wx_ref, bx_ref, lam_ref, o_ref, ext_scr, hc_scr, *, ts):
    W = LRU_WIDTH
    pad = 8

    @pl.when(pl.program_id(1) == 0)
    def _():
        ext_scr[0:pad, :] = jnp.zeros((pad, W), F32)
        hc_scr[...] = jnp.zeros((1, W), F32)

    u = u_ref[:, 0:W]
    ug = u_ref[:, W:2 * W]
    ext_scr[pad:pad + ts, :] = u
    xc = cb_ref[...] + ext_scr[pl.ds(pad - 3, ts), :] * cw_ref[0:1, :]
    xc = xc + ext_scr[pl.ds(pad - 2, ts), :] * cw_ref[1:2, :]
    xc = xc + ext_scr[pl.ds(pad - 1, ts), :] * cw_ref[2:3, :]
    xc = xc + u * cw_ref[3:4, :]
    ext_scr[0:pad, :] = u[ts - pad:ts, :]

    xb = xc.astype(BF16)
    r = jax.nn.sigmoid(jnp.dot(xb, wa_ref[...], preferred_element_type=F32) + ba_ref[...])
    ig = jax.nn.sigmoid(jnp.dot(xb, wx_ref[...], preferred_element_type=F32) + bx_ref[...])
    log_a = (-LRU_C) * r * _softplus(-lam_ref[...])
    a = jnp.exp(log_a)
    y2 = 2.0 * log_a
    b = jnp.sqrt(jnp.tanh(-0.5 * y2) * (jnp.exp(y2) + 1.0)) * (ig * xc)

    row = lax.broadcasted_iota(jnp.int32, (ts, W), 0)
    d = 1
    while d < ts:
        keep = row >= d
        a_sh = jnp.where(keep, pltpu.roll(a, d, 0), 1.0)
        b_sh = jnp.where(keep, pltpu.roll(b, d, 0), 0.0)
        b = a * b_sh + b
        a = a * a_sh
        d *= 2
    h = b + a * hc_scr[...]
    hc_scr[...] = h[ts - 1:ts, :]
    o_ref[...] = (h * jax.nn.gelu(ug)).astype(o_ref.dtype)


def _lru_branch(u, conv_w, conv_b, wa_bd, ba, wx_bd, bx, lam, B, S, ts=256):
    W = LRU_WIDTH
    row = lambda v: v.reshape(1, W)
    full = lambda shape: pl.BlockSpec(shape, lambda b, t: (0,) * len(shape))
    return pl.pallas_call(
        functools.partial(_lru_kernel, ts=ts),
        grid=(B, S // ts),
        in_specs=[
            pl.BlockSpec((None, ts, 2 * W), lambda b, t: (b, t, 0)),
            full((CONV_WIDTH, W)), full((1, W)),
            full((W, W)), full((1, W)), full((W, W)), full((1, W)), full((1, W)),
        ],
        out_specs=pl.BlockSpec((None, ts, W), lambda b, t: (b, t, 0)),
        out_shape=jax.ShapeDtypeStruct((B, S, W), BF16),
        scratch_shapes=[pltpu.VMEM((ts + 8, W), F32), pltpu.VMEM((1, W), F32)],
        compiler_params=_cparams(("parallel", "arbitrary")),
        name="lru_scan",
    )(u.reshape(B, S, 2 * W), conv_w, row(conv_b), wa_bd, row(ba), wx_bd, row(bx), row(lam))


def _rope_slot(x, c, s1, s2):
    return x * c + pltpu.roll(x, LANES - MLA_ROPE // 2, 1) * s1 + pltpu.roll(x, MLA_ROPE // 2, 1) * s2


def _prep_kernel(lat_ref, cq_t_ref, ck_t_ref, s1_ref, s2_ref, qn_ref, kvn_ref, wq_ref, wk_ref, wv_ref, bf_ref,
                 q_ref, k_ref, v_ref, crow_ref, ccol_ref, carry_scr, *, ts, q_scale):
    @pl.when(pl.program_id(1) == 0)
    def _():
        carry_scr[...] = jnp.zeros((1, LANES), F32)

    cq_t, ck_t, s1, s2 = cq_t_ref[...], ck_t_ref[...], s1_ref[...], s2_ref[...]

    cqn = _rms(lat_ref[:, 0:MLA_Q_LORA], qn_ref[...]).astype(BF16)
    q = jnp.dot(cqn, wq_ref[...], preferred_element_type=F32)
    for h in range(MLA_HEADS):
        sl = slice(h * HEAD_SLOT, (h + 1) * HEAD_SLOT)
        q_ref[:, sl] = (_rope_slot(q[:, sl], cq_t, s1, s2) * q_scale).astype(q_ref.dtype)

    ckvn = _rms(lat_ref[:, MLA_Q_LORA:MLA_Q_LORA + MLA_KV_LORA], kvn_ref[...]).astype(BF16)
    kn = jnp.dot(ckvn, wk_ref[...], preferred_element_type=F32)
    v_ref[...] = jnp.dot(ckvn, wv_ref[...], preferred_element_type=F32).astype(v_ref.dtype)
    misc = lat_ref[:, MLA_Q_LORA + MLA_KV_LORA:MLA_Q_LORA + MLA_KV_LORA + LANES]
    kr = _rope_slot(misc, ck_t, s1, s2)
    for h in range(MLA_HEADS):
        sl = slice(h * HEAD_SLOT, (h + 1) * HEAD_SLOT)
        k_ref[:, sl] = (kn[:, sl] + kr).astype(k_ref.dtype)

    c = -_softplus(-(misc + bf_ref[...]))
    row = lax.broadcasted_iota(jnp.int32, (ts, LANES), 0)
    d = 1
    while d < ts:
        c = c + jnp.where(row >= d, pltpu.roll(c, d, 0), 0.0)
        d *= 2
    c = c + carry_scr[...]
    carry_scr[...] = c[ts - 1:ts, :]
    c = c * LOG2E
    crow_ref[...] = c.T[FLOGIT_LANE0:FLOGIT_LANE0 + FOX_HEADS, :]
    lane = lax.broadcasted_iota(jnp.int32, (ts, LANES), 1)
    for pr in range(FOX_HEADS // 2):
        l0 = FLOGIT_LANE0 + 2 * pr
        ccol_ref[pr] = jnp.where(lane < FOX_HEAD_DIM, c[:, l0:l0 + 1], c[:, l0 + 1:l0 + 2])


def _prep(lat, tabs, qn, kvn, wq, wk, wv, bf_row, B, S, q_scale, ts=512):
    NL = lat.shape[-1]
    full = lambda shape: pl.BlockSpec(shape, lambda b, t: (0,) * len(shape))
    tab = pl.BlockSpec((ts, LANES), lambda b, t: (t, 0))
    HS = MLA_HEADS * HEAD_SLOT
    return pl.pallas_call(
        functools.partial(_prep_kernel, ts=ts, q_scale=q_scale),
        grid=(B, S // ts),
        in_specs=[
            pl.BlockSpec((None, ts, NL), lambda b, t: (b, t, 0)),
            tab, tab, tab, tab,
            full((1, MLA_Q_LORA)), full((1, MLA_KV_LORA)),
            full((MLA_Q_LORA, HS)), full((MLA_KV_LORA, HS)), full((MLA_KV_LORA, MLA_HEADS * MLA_V)),
            full((1, LANES)),
        ],
        out_specs=[
            pl.BlockSpec((None, ts, HS), lambda b, t: (b, t, 0)),
            pl.BlockSpec((None, ts, HS), lambda b, t: (b, t, 0)),
            pl.BlockSpec((None, ts, MLA_HEADS * MLA_V), lambda b, t: (b, t, 0)),
            pl.BlockSpec((None, FOX_HEADS, ts), lambda b, t: (b, 0, t)),
            pl.BlockSpec((None, FOX_HEADS // 2, ts, LANES), lambda b, t: (b, 0, t, 0)),
        ],
        out_shape=[
            jax.ShapeDtypeStruct((B, S, HS), BF16),
            jax.ShapeDtypeStruct((B, S, HS), BF16),
            jax.ShapeDtypeStruct((B, S, MLA_HEADS * MLA_V), BF16),
            jax.ShapeDtypeStruct((B, FOX_HEADS, S), F32),
            jax.ShapeDtypeStruct((B, FOX_HEADS // 2, S, LANES), F32),
        ],
        scratch_shapes=[pltpu.VMEM((1, LANES), F32)],
        compiler_params=_cparams(("parallel", "arbitrary")),
        name="mla_fox_prep",
    )(lat.reshape(B, S, NL), *tabs, qn.reshape(1, -1), kvn.reshape(1, -1), wq, wk, wv, bf_row)


def _attn_kernel(*refs, tq, dk, dv, unit, has_decay):
    if has_decay:
        q_ref, k_ref, v_ref, crow_ref, ccol_ref, o_ref, m_scr, l_scr, acc_scr = refs
    else:
        q_ref, k_ref, v_ref, o_ref, m_scr, l_scr, acc_scr = refs
    tk = tq
    qi = pl.program_id(2)
    lane = lax.broadcasted_iota(jnp.int32, (1, 2 * dv), 1)
    first = lane < dv

    m_scr[...] = jnp.full(m_scr.shape, NEG_INF, F32)
    l_scr[...] = jnp.zeros(l_scr.shape, F32)
    acc_scr[...] = jnp.zeros(acc_scr.shape, F32)

    if dk == HEAD_SLOT:
        qs = [q_ref[:, h * dk:(h + 1) * dk] for h in range(2)]
    else:
        qall = q_ref[...]
        zero = jnp.zeros_like(qall)
        qs = [jnp.where(first, qall, zero), jnp.where(first, zero, qall)]
    if has_decay:
        cq = [ccol_ref[:, 0:1], ccol_ref[:, dv:dv + 1]]

    def step(j, masked):
        k0 = pl.multiple_of(j * tk, tk)
        vblk = v_ref[pl.ds(k0, tk), :]
        pvs, alphas = [], []
        for h in range(2):
            if dk == HEAD_SLOT:
                kh = k_ref[pl.ds(k0, tk), h * dk:(h + 1) * dk]
            else:
                kh = k_ref[pl.ds(k0, tk), :]
            s = lax.dot_general(qs[h], kh, (((1,), (1,)), ((), ())), preferred_element_type=F32)
            if has_decay:
                s = s + cq[h] - crow_ref[h:h + 1, pl.ds(k0, tk)]
            if masked:
                r_i = lax.broadcasted_iota(jnp.int32, (tq, tk), 0)
                c_i = lax.broadcasted_iota(jnp.int32, (tq, tk), 1)
                if unit > 1:
                    sh = unit.bit_length() - 1
                    r_i, c_i = r_i >> sh, c_i >> sh
                s = jnp.where(r_i >= c_i, s, NEG_INF)
            m_prev = m_scr[h]
            m_new = jnp.maximum(m_prev, jnp.max(s, axis=-1, keepdims=True))
            alpha = jnp.exp2(m_prev - m_new)
            p = jnp.exp2(s - m_new)
            l_scr[h] = alpha * l_scr[h] + jnp.sum(p, axis=-1, keepdims=True)
            m_scr[h] = m_new
            pvs.append(jnp.dot(p.astype(BF16), vblk, preferred_element_type=F32))
            alphas.append(alpha)
        acc_scr[...] = jnp.where(first, alphas[0], alphas[1]) * acc_scr[...] + jnp.where(first, pvs[0], pvs[1])

    def body(j, carry):
        step(j, False)
        return carry

    lax.fori_loop(0, qi, body, 0)
    step(qi, True)
    o_ref[...] = (acc_scr[...] / jnp.where(first, l_scr[0], l_scr[1])).astype(o_ref.dtype)


def _attention(q, k, v, B, S, *, dk, unit, offs=(0, 0, 0), crow=None, ccol=None, tq=512):
    dv = MLA_V
    npair = MLA_HEADS // 2
    has_decay = crow is not None
    qw = 2 * dk if dk == HEAD_SLOT else LANES
    qo, ko, vo = offs
    in_specs = [
        pl.BlockSpec((None, tq, qw), lambda b, p, i: (b, i, qo + p)),
        pl.BlockSpec((None, S, qw), lambda b, p, i: (b, 0, ko + p)),
        pl.BlockSpec((None, S, 2 * dv), lambda b, p, i: (b, 0, vo + p)),
    ]
    args = [q, k, v]
    if has_decay:
        in_specs += [
            pl.BlockSpec((None, None, 2, S), lambda b, p, i: (b, p, 0, 0)),
            pl.BlockSpec((None, None, tq, LANES), lambda b, p, i: (b, p, i, 0)),
        ]
        args += [crow.reshape(B, npair, 2, S), ccol]
    return pl.pallas_call(
        functools.partial(_attn_kernel, tq=tq, dk=dk, dv=dv, unit=unit, has_decay=has_decay),
        grid=(B, npair, S // tq),
        in_specs=in_specs,
        out_specs=pl.BlockSpec((None, tq, 2 * dv), lambda b, p, i: (b, i, p)),
        out_shape=jax.ShapeDtypeStruct((B, S, npair * 2 * dv), BF16),
        scratch_shapes=[pltpu.VMEM((2, tq, 1), F32), pltpu.VMEM((2, tq, 1), F32), pltpu.VMEM((tq, 2 * dv), F32)],
        compiler_params=_cparams(("parallel", "parallel", "arbitrary")),
        name="fox_attention" if has_decay else "mla_attention",
    )(*args)


def _merge_kernel(ya_ref, ob_ref, oc_ref, g_ref, x_ref, wa_ref, wb_ref, wc_ref, wo_ref, o_ref):
    D = x_ref.shape[-1]
    ya = jnp.dot(ya_ref[...], wa_ref[...], preferred_element_type=F32)
    yb = jnp.dot(ob_ref[...], wb_ref[...], preferred_element_type=F32)
    yc = jnp.dot(oc_ref[...], wc_ref[...], preferred_element_type=F32)
    merged = g_ref[:, 0:D] * ya + g_ref[:, D:2 * D] * yb + g_ref[:, 2 * D:3 * D] * yc
    o_ref[...] = x_ref[...] + jnp.dot(merged.astype(BF16), wo_ref[...], preferred_element_type=F32)


def _merge(ya, ob, oc, g, x, wa, wb, wc, wo, tm=256):
    T, D = x.shape
    W = ya.shape[-1]
    rows = lambda n: pl.BlockSpec((tm, n), lambda i: (i, 0))
    full = lambda shape: pl.BlockSpec(shape, lambda i: (0, 0))
    return pl.pallas_call(
        _merge_kernel,
        grid=(T // tm,),
        in_specs=[rows(W), rows(W), rows(W), rows(3 * D), rows(D),
                  full((W, D)), full((W, D)), full((W, D)), full((D, D))],
        out_specs=rows(D),
        out_shape=jax.ShapeDtypeStruct((T, D), F32),
        compiler_params=_cparams(("parallel",)),
        name="merge",
    )(ya, ob, oc, g, x, wa, wb, wc, wo)


def _ffn_kernel(x_ref, g_ref, wg_ref, wu_ref, wd_ref, o_ref, h_scr, acc_scr):
    j = pl.program_id(1)
    last = pl.num_programs(1) - 1

    @pl.when(j == 0)
    def _():
        h_scr[...] = _rms(x_ref[...], g_ref[...]).astype(BF16)
        acc_scr[...] = jnp.zeros(acc_scr.shape, F32)

    h = h_scr[...]
    gate = jnp.dot(h, wg_ref[...], preferred_element_type=F32)
    up = jnp.dot(h, wu_ref[...], preferred_element_type=F32)
    act = (jax.nn.silu(gate) * up).astype(BF16)
    acc_scr[...] += jnp.dot(act, wd_ref[...], preferred_element_type=F32)

    @pl.when(j == last)
    def _():
        o_ref[...] = x_ref[...] + acc_scr[...]


def _ffn(x, g, w_gate_up, w_down, tm=512, nf=2):
    T, D = x.shape
    d_ff = w_down.shape[0]
    tf = d_ff // nf
    return pl.pallas_call(
        _ffn_kernel,
        grid=(T // tm, nf),
        in_specs=[
            pl.BlockSpec((tm, D), lambda i, j: (i, 0)),
            pl.BlockSpec((1, D), lambda i, j: (0, 0)),
            pl.BlockSpec((D, tf), lambda i, j: (0, j)),
            pl.BlockSpec((D, tf), lambda i, j: (0, nf + j)),
            pl.BlockSpec((tf, D), lambda i, j: (j, 0)),
        ],
        out_specs=pl.BlockSpec((tm, D), lambda i, j: (i, 0)),
        out_shape=jax.ShapeDtypeStruct((T, D), F32),
        scratch_shapes=[pltpu.VMEM((tm, D), BF16), pltpu.VMEM((tm, D), F32)],
        compiler_params=_cparams(("parallel", "arbitrary")),
        name="swiglu",
    )(x, g.reshape(1, D), w_gate_up, w_gate_up, w_down)


def _ple_kernel(x_ref, p_ref, g_ref, wpg_ref, wple_ref, fn_ref, o_ref, *, final):
    x = x_ref[...]
    hn = _rms(x, g_ref[...]).astype(BF16)
    pg = jax.nn.sigmoid(jnp.dot(hn, wpg_ref[...], preferred_element_type=F32))
    pe = jnp.dot(p_ref[...].astype(BF16), wple_ref[...], preferred_element_type=F32)
    xn = x + pg * pe
    if final:
        xn = _rms(xn, fn_ref[...])
    o_ref[...] = xn


def _ple(x, p, g, wpg, wple, fn, final, tm=512):
    T, D = x.shape
    P = p.shape[-1]
    full = lambda shape: pl.BlockSpec(shape, lambda i: (0, 0))
    return pl.pallas_call(
        functools.partial(_ple_kernel, final=final),
        grid=(T // tm,),
        in_specs=[pl.BlockSpec((tm, D), lambda i: (i, 0)), pl.BlockSpec((tm, P), lambda i: (i, 0)),
                  full((1, D)), full((D, D)), full((P, D)), full((1, D))],
        out_specs=pl.BlockSpec((tm, D), lambda i: (i, 0)),
        out_shape=jax.ShapeDtypeStruct((T, D), F32),
        compiler_params=_cparams(("parallel",)),
        name="ple_final" if final else "ple",
    )(x, p, g.reshape(1, D), wpg, wple, fn.reshape(1, D))


def _block_diag(w):
    H, I, J = w.shape
    eye = jnp.eye(H, dtype=w.dtype)
    return (eye[:, None, :, None] * w[:, :, None, :]).reshape(H * I, H * J)


def _head_slots(w, width):
    K = w.shape[0]
    w = w.reshape(K, MLA_HEADS, width)
    return jnp.pad(w, ((0, 0), (0, 0), (0, HEAD_SLOT - width))).reshape(K, MLA_HEADS * HEAD_SLOT)


def _rope_tables(S):
    pos = jnp.arange(S, dtype=F32)
    inv_freq = ROPE_BASE ** (-jnp.arange(0, MLA_ROPE, 2, dtype=F32) / MLA_ROPE)
    ang = pos[:, None] * inv_freq[None, :]
    cos, sin = jnp.cos(ang), jnp.sin(ang)
    half = MLA_ROPE // 2
    z = lambda n: jnp.zeros((S, n), F32)
    tail = LANES - ROPE_LANE0 - MLA_ROPE
    cq_t = jnp.concatenate([jnp.ones((S, ROPE_LANE0), F32), cos, cos, z(tail)], axis=1)
    ck_t = jnp.concatenate([z(ROPE_LANE0), cos, cos, z(tail)], axis=1)
    s1 = jnp.concatenate([z(ROPE_LANE0), -sin, z(half), z(tail)], axis=1)
    s2 = jnp.concatenate([z(ROPE_LANE0), z(half), sin, z(tail)], axis=1)
    return cq_t, ck_t, s1, s2


def kernel(x, p, mix_norm, w_in, gate_b, conv_w, conv_b, lru_wa, lru_ba, lru_wx, lru_bx, lru_lambda,
           mla_q_norm, mla_wuq, mla_kv_norm, mla_wukv, fox_bf, w_br_a, w_br_b, w_br_c, w_o,
           ffn_norm, w_gate_up, w_down, ple_norm, w_ple_gate, w_ple, final_norm):
    B, S, D = x.shape
    depth = w_in.shape[0]
    T = B * S
    tabs = _rope_tables(S)
    mla_scale = (MLA_NOPE + MLA_ROPE) ** -0.5 * LOG2E
    fox_scale = FOX_HEAD_DIM ** -0.5 * LOG2E

    o_u = 0
    o_cq = 2 * LRU_WIDTH
    o_ckv = o_cq + MLA_Q_LORA
    o_kr = o_ckv + MLA_KV_LORA
    o_fq = o_kr + MLA_ROPE
    o_fl = o_fq + 3 * FOX_WIDTH
    o_g = o_fl + FOX_HEADS

    xf = x.reshape(T, D)
    for i in range(depth):
        wi = w_in[i]
        w_u = wi[:, o_u:o_cq].astype(BF16)
        misc_cols = jnp.concatenate([
            jnp.zeros((D, ROPE_LANE0), F32), wi[:, o_kr:o_fq], wi[:, o_fl:o_g],
            jnp.zeros((D, LANES - ROPE_LANE0 - MLA_ROPE - FOX_HEADS), F32)], axis=1)
        w_lat = jnp.concatenate([wi[:, o_cq:o_kr], misc_cols], axis=1).astype(BF16)
        w_fox = wi[:, o_fq:o_fl].astype(BF16)
        w_g = wi[:, o_g:].astype(BF16)
        fox_cs = jnp.concatenate([jnp.full((FOX_WIDTH,), fox_scale, F32), jnp.ones((2 * FOX_WIDTH,), F32)])

        u = _norm_matmul(xf, mix_norm[i], w_u, name="in_lru", out_dtype=F32, tn=512)
        lat = _norm_matmul(xf, mix_norm[i], w_lat, name="in_latent", out_dtype=F32, tn=384)
        fox = _norm_matmul(xf, mix_norm[i], w_fox, name="in_fox", out_dtype=BF16, tn=512, col_scale=fox_cs)
        g = _norm_matmul(xf, mix_norm[i], w_g, name="in_gates", out_dtype=F32, tn=512, col_bias=gate_b[i],
                         act="sigmoid")

        ya = _lru_branch(u, conv_w[i], conv_b[i], _block_diag(lru_wa[i]).astype(BF16), lru_ba[i],
                         _block_diag(lru_wx[i]).astype(BF16), lru_bx[i], lru_lambda[i], B, S)

        wq = _head_slots(mla_wuq[i], MLA_NOPE + MLA_ROPE).astype(BF16)
        wukv = mla_wukv[i].reshape(MLA_KV_LORA, MLA_HEADS, MLA_NOPE + MLA_V)
        wk = _head_slots(wukv[:, :, :MLA_NOPE].reshape(MLA_KV_LORA, MLA_HEADS * MLA_NOPE), MLA_NOPE).astype(BF16)
        wv = wukv[:, :, MLA_NOPE:].reshape(MLA_KV_LORA, MLA_HEADS * MLA_V).astype(BF16)
        bf_row = jnp.zeros((1, LANES), F32).at[0, FLOGIT_LANE0:FLOGIT_LANE0 + FOX_HEADS].set(fox_bf[i])
        q_m, k_m, v_m, crow, ccol = _prep(lat, tabs, mla_q_norm[i], mla_kv_norm[i], wq, wk, wv, bf_row,
                                          B, S, mla_scale)

        ob = _attention(q_m, k_m, v_m, B, S, dk=HEAD_SLOT, unit=MLA_CHUNK)
        fox3 = fox.reshape(B, S, 3 * FOX_WIDTH)
        npair = FOX_HEADS // 2
        oc = _attention(fox3, fox3, fox3, B, S, dk=FOX_HEAD_DIM, unit=1, offs=(0, npair, 2 * npair),
                        crow=crow, ccol=ccol)

        xf = _merge(ya.reshape(T, -1), ob.reshape(T, -1), oc.reshape(T, -1), g, xf,
                    w_br_a[i].astype(BF16), w_br_b[i].astype(BF16), w_br_c[i].astype(BF16), w_o[i].astype(BF16))
        xf = _ffn(xf, ffn_norm[i], w_gate_up[i].astype(BF16), w_down[i].astype(BF16))
        xf = _ple(xf, p[i].reshape(T, -1), ple_norm[i], w_ple_gate[i].astype(BF16), w_ple[i].astype(BF16),
                  final_norm, final=(i == depth - 1))
    return xf.reshape(B, S, D)
```

```python
import functools
import math

import jax
import jax.numpy as jnp
from jax import lax
from jax.experimental import pallas as pl
from jax.experimental.pallas import tpu as pltpu

F32 = jnp.float32
BF16 = jnp.bfloat16

EPS = 1e-6
NEG_INF = -1e30
LOG2E = math.log2(math.e)

LRU_WIDTH = 512
LRU_HEADS = 8
CONV_WIDTH = 4
LRU_C = 8.0

MLA_HEADS = 8
MLA_Q_LORA = 384
MLA_KV_LORA = 256
MLA_NOPE = 64
MLA_ROPE = 32
MLA_V = 64
ROPE_BASE = 10000.0
MLA_CHUNK = 64

FOX_HEADS = 8
FOX_HEAD_DIM = 64
FOX_WIDTH = FOX_HEADS * FOX_HEAD_DIM

LANES = 128
HEAD_SLOT = LANES
ROPE_LANE0 = MLA_NOPE
FLOGIT_LANE0 = MLA_NOPE + MLA_ROPE
V_ROWS = 80
Q_SUB = 256

VMEM_LIMIT = 56 * 1024 * 1024


def _cparams(sem):
    return pltpu.CompilerParams(dimension_semantics=sem, vmem_limit_bytes=VMEM_LIMIT)


def _rms(xf, g):
    ms = jnp.mean(xf * xf, axis=-1, keepdims=True)
    return xf * lax.rsqrt(ms + EPS) * g


def _softplus(z):
    return jnp.maximum(z, 0.0) + jnp.log1p(jnp.exp(-jnp.abs(z)))


def _norm_matmul_kernel(x_ref, g_ref, w_ref, cs_ref, cb_ref, o_ref, h_scr, *, use_scale, use_bias, act):
    @pl.when(pl.program_id(1) == 0)
    def _():
        h_scr[...] = _rms(x_ref[...], g_ref[...]).astype(BF16)

    acc = jnp.dot(h_scr[...], w_ref[...], preferred_element_type=F32)
    if use_scale:
        acc = acc * cs_ref[...]
    if use_bias:
        acc = acc + cb_ref[...]
    if act == "sigmoid":
        acc = jax.nn.sigmoid(acc)
    o_ref[...] = acc.astype(o_ref.dtype)


def _norm_matmul(x, g, w, *, name, out_dtype, tn, tm=1024, col_scale=None, col_bias=None, act=None):
    T, D = x.shape
    N = w.shape[1]
    ones = jnp.ones((1, N), F32)
    cs = ones if col_scale is None else col_scale.reshape(1, N).astype(F32)
    cb = ones if col_bias is None else col_bias.reshape(1, N).astype(F32)
    kern = functools.partial(_norm_matmul_kernel, use_scale=col_scale is not None,
                             use_bias=col_bias is not None, act=act)
    return pl.pallas_call(
        kern,
        grid=(T // tm, N // tn),
        in_specs=[
            pl.BlockSpec((tm, D), lambda i, j: (i, 0)),
            pl.BlockSpec((1, D), lambda i, j: (0, 0)),
            pl.BlockSpec((D, tn), lambda i, j: (0, j)),
            pl.BlockSpec((1, tn), lambda i, j: (0, j)),
            pl.BlockSpec((1, tn), lambda i, j: (0, j)),
        ],
        out_specs=pl.BlockSpec((tm, tn), lambda i, j: (i, j)),
        out_shape=jax.ShapeDtypeStruct((T, N), out_dtype),
        scratch_shapes=[pltpu.VMEM((tm, D), BF16)],
        compiler_params=_cparams(("parallel", "arbitrary")),
        name=name,
    )(x, g.reshape(1, D), w, cs, cb)


def _lru_kernel(u_ref, cw_ref, cb_ref, wa_ref, ba_ref, wx_ref, bx_ref, lam_ref, o_ref, ext_scr, hc_scr, *, ts):
    W = LRU_WIDTH
    pad = 8

    @pl.when(pl.program_id(1) == 0)
    def _():
        ext_scr[0:pad, :] = jnp.zeros((pad, W), F32)
        hc_scr[...] = jnp.zeros((1, W), F32)

    u = u_ref[:, 0:W]
    ug = u_ref[:, W:2 * W]
    ext_scr[pad:pad + ts, :] = u
    xc = cb_ref[...] + ext_scr[pl.ds(pad - 3, ts), :] * cw_ref[0:1, :]
    xc = xc + ext_scr[pl.ds(pad - 2, ts), :] * cw_ref[1:2, :]
    xc = xc + ext_scr[pl.ds(pad - 1, ts), :] * cw_ref[2:3, :]
    xc = xc + u * cw_ref[3:4, :]
    ext_scr[0:pad, :] = u[ts - pad:ts, :]

    xb = xc.astype(BF16)
    r = jax.nn.sigmoid(jnp.dot(xb, wa_ref[...], preferred_element_type=F32) + ba_ref[...])
    ig = jax.nn.sigmoid(jnp.dot(xb, wx_ref[...], preferred_element_type=F32) + bx_ref[...])
    log_a = (-LRU_C) * r * _softplus(-lam_ref[...])
    a = jnp.exp(log_a)
    y2 = 2.0 * log_a
    b = jnp.sqrt(jnp.tanh(-0.5 * y2) * (jnp.exp(y2) + 1.0)) * (ig * xc)

    row = lax.broadcasted_iota(jnp.int32, (ts, W), 0)
    d = 1
    while d < ts:
        keep = row >= d
        a_sh = jnp.where(keep, pltpu.roll(a, d, 0), 1.0)
        b_sh = jnp.where(keep, pltpu.roll(b, d, 0), 0.0)
        b = a * b_sh + b
        a = a * a_sh
        d *= 2
    h = b + a * hc_scr[...]
    hc_scr[...] = h[ts - 1:ts, :]
    o_ref[...] = (h * jax.nn.gelu(ug)).astype(o_ref.dtype)


def _lru_branch(u, conv_w, conv_b, wa_bd, ba, wx_bd, bx, lam, B, S, ts=256):
    W = LRU_WIDTH
    row = lambda v: v.reshape(1, W)
    full = lambda shape: pl.BlockSpec(shape, lambda b, t: (0,) * len(shape))
    return pl.pallas_call(
        functools.partial(_lru_kernel, ts=ts),
        grid=(B, S // ts),
        in_specs=[
            pl.BlockSpec((None, ts, 2 * W), lambda b, t: (b, t, 0)),
            full((CONV_WIDTH, W)), full((1, W)),
            full((W, W)), full((1, W)), full((W, W)), full((1, W)), full((1, W)),
        ],
        out_specs=pl.BlockSpec((None, ts, W), lambda b, t: (b, t, 0)),
        out_shape=jax.ShapeDtypeStruct((B, S, W), BF16),
        scratch_shapes=[pltpu.VMEM((ts + 8, W), F32), pltpu.VMEM((1, W), F32)],
        compiler_params=_cparams(("parallel", "arbitrary")),
        name="lru_scan",
    )(u.reshape(B, S, 2 * W), conv_w, row(conv_b), wa_bd, row(ba), wx_bd, row(bx), row(lam))


def _rope_slot(x, c, s1, s2):
    return x * c + pltpu.roll(x, LANES - MLA_ROPE // 2, 1) * s1 + pltpu.roll(x, MLA_ROPE // 2, 1) * s2


def _bf16_terms(c):
    hi = c.astype(BF16).astype(F32)
    r = c - hi
    lo = r.astype(BF16).astype(F32)
    return hi, lo, (r - lo).astype(BF16).astype(F32)


def _store_values_t(vt_ref, v, ts):
    vt = v.T
    row = lax.broadcasted_iota(jnp.int32, (V_ROWS - MLA_V, ts), 0)
    ones_tile = jnp.where(row == 0, 1.0, 0.0).astype(vt_ref.dtype)
    for h in range(MLA_HEADS):
        vt_ref[h * V_ROWS:h * V_ROWS + MLA_V, :] = vt[h * MLA_V:(h + 1) * MLA_V, :].astype(vt_ref.dtype)
        vt_ref[h * V_ROWS + MLA_V:(h + 1) * V_ROWS, :] = ones_tile


def _prep_kernel(lat_ref, fox_ref, cq_t_ref, ck_t_ref, s1_ref, s2_ref, qn_ref, kvn_ref, wq_ref, wk_ref, wv_ref,
                 bf_ref, qm_ref, km_ref, vtm_ref, qf_ref, kf_ref, vtf_ref, carry_scr, *, ts, q_scale):
    @pl.when(pl.program_id(1) == 0)
    def _():
        carry_scr[...] = jnp.zeros((1, LANES), F32)

    cq_t, ck_t, s1, s2 = cq_t_ref[...], ck_t_ref[...], s1_ref[...], s2_ref[...]

    cqn = _rms(lat_ref[:, 0:MLA_Q_LORA], qn_ref[...]).astype(BF16)
    q = jnp.dot(cqn, wq_ref[...], preferred_element_type=F32)
    for h in range(MLA_HEADS):
        sl = slice(h * HEAD_SLOT, (h + 1) * HEAD_SLOT)
        qm_ref[:, sl] = (_rope_slot(q[:, sl], cq_t, s1, s2) * q_scale).astype(qm_ref.dtype)

    ckvn = _rms(lat_ref[:, MLA_Q_LORA:MLA_Q_LORA + MLA_KV_LORA], kvn_ref[...]).astype(BF16)
    kn = jnp.dot(ckvn, wk_ref[...], preferred_element_type=F32)
    misc = lat_ref[:, MLA_Q_LORA + MLA_KV_LORA:MLA_Q_LORA + MLA_KV_LORA + LANES]
    kr = _rope_slot(misc, ck_t, s1, s2)
    for h in range(MLA_HEADS):
        sl = slice(h * HEAD_SLOT, (h + 1) * HEAD_SLOT)
        km_ref[:, sl] = (kn[:, sl] + kr).astype(km_ref.dtype)
    _store_values_t(vtm_ref, jnp.dot(ckvn, wv_ref[...], preferred_element_type=F32), ts)

    c = -_softplus(-(misc + bf_ref[...]))
    row = lax.broadcasted_iota(jnp.int32, (ts, LANES), 0)
    d = 1
    while d < ts:
        c = c + jnp.where(row >= d, pltpu.roll(c, d, 0), 0.0)
        d *= 2
    c = c + carry_scr[...]
    carry_scr[...] = c[ts - 1:ts, :]
    c = c * LOG2E

    lane = lax.broadcasted_iota(jnp.int32, (ts, LANES), 1) - FOX_HEAD_DIM
    nslot = FOX_HEADS * HEAD_SLOT
    for h in range(FOX_HEADS):
        sl = slice(h * HEAD_SLOT, (h + 1) * HEAD_SLOT)
        hi, lo, lo2 = _bf16_terms(jnp.broadcast_to(c[:, FLOGIT_LANE0 + h:FLOGIT_LANE0 + h + 1], (ts, LANES)))
        q_bias = jnp.where(lane == 0, hi, jnp.where(lane == 1, lo, jnp.where(lane == 2, lo2,
                           jnp.where((lane >= 3) & (lane < 6), 1.0, 0.0))))
        k_bias = jnp.where(lane == 3, -hi, jnp.where(lane == 4, -lo, jnp.where(lane == 5, -lo2,
                           jnp.where((lane >= 0) & (lane < 3), 1.0, 0.0))))
        qf_ref[:, sl] = (fox_ref[:, sl].astype(F32) + q_bias).astype(qf_ref.dtype)
        kf_ref[:, sl] = (fox_ref[:, nslot + h * HEAD_SLOT:nslot + (h + 1) * HEAD_SLOT].astype(F32)
                         + k_bias).astype(kf_ref.dtype)
    _store_values_t(vtf_ref, fox_ref[:, 2 * nslot:2 * nslot + FOX_WIDTH].astype(F32), ts)


def _prep(lat, fox, tabs, qn, kvn, wq, wk, wv, bf_row, B, S, q_scale, ts=512):
    NL, NF = lat.shape[-1], fox.shape[-1]
    full = lambda shape: pl.BlockSpec(shape, lambda b, t: (0,) * len(shape))
    tab = pl.BlockSpec((ts, LANES), lambda b, t: (t, 0))
    HS = MLA_HEADS * HEAD_SLOT
    slots = pl.BlockSpec((None, ts, HS), lambda b, t: (b, t, 0))
    vts = pl.BlockSpec((None, MLA_HEADS * V_ROWS, ts), lambda b, t: (b, 0, t))
    slot_shape = jax.ShapeDtypeStruct((B, S, HS), BF16)
    vt_shape = jax.ShapeDtypeStruct((B, MLA_HEADS * V_ROWS, S), BF16)
    return pl.pallas_call(
        functools.partial(_prep_kernel, ts=ts, q_scale=q_scale),
        grid=(B, S // ts),
        in_specs=[
            pl.BlockSpec((None, ts, NL), lambda b, t: (b, t, 0)),
            pl.BlockSpec((None, ts, NF), lambda b, t: (b, t, 0)),
            tab, tab, tab, tab,
            full((1, MLA_Q_LORA)), full((1, MLA_KV_LORA)),
            full((MLA_Q_LORA, HS)), full((MLA_KV_LORA, HS)), full((MLA_KV_LORA, MLA_HEADS * MLA_V)),
            full((1, LANES)),
        ],
        out_specs=[slots, slots, vts, slots, slots, vts],
        out_shape=[slot_shape, slot_shape, vt_shape, slot_shape, slot_shape, vt_shape],
        scratch_shapes=[pltpu.VMEM((1, LANES), F32)],
        compiler_params=_cparams(("parallel", "arbitrary")),
        name="mla_fox_prep",
    )(lat.reshape(B, S, NL), fox.reshape(B, S, NF), *tabs, qn.reshape(1, -1), kvn.reshape(1, -1), wq, wk, wv,
      bf_row)


def _attn_kernel(q_ref, k_ref, vt_ref, o_ref, m_scr, acc_scr, *, tq, tk, unit):
    qi = pl.program_id(2)
    nsub = tq // Q_SUB
    m_scr[...] = jnp.full(m_scr.shape, NEG_INF, F32)
    acc_scr[...] = jnp.zeros(acc_scr.shape, F32)

    def chunk(k0, kinds):
        streams = [(h, c) for c in range(nsub) for h in range(2) if kinds[c] is not None]
        logits = []
        for h, c in streams:
            kh = k_ref[pl.ds(k0, tk), h * HEAD_SLOT:(h + 1) * HEAD_SLOT]
            qh = q_ref[c * Q_SUB:(c + 1) * Q_SUB, h * HEAD_SLOT:(h + 1) * HEAD_SLOT]
            logits.append(lax.dot_general(kh, qh, (((1,), (1,)), ((), ())), preferred_element_type=F32))
        for (h, c), st in zip(streams, logits):
            cs = slice(c * Q_SUB, (c + 1) * Q_SUB)
            if kinds[c] != 0:
                kk = lax.broadcasted_iota(jnp.int32, (tk, Q_SUB), 0) + kinds[c][0]
                qq = lax.broadcasted_iota(jnp.int32, (tk, Q_SUB), 1) + c * Q_SUB
                if unit > 1:
                    sh = unit.bit_length() - 1
                    kk, qq = kk >> sh, qq >> sh
                st = jnp.where(qq >= kk, st, NEG_INF)
            m_prev = m_scr[h, :, cs]
            m_new = jnp.maximum(m_prev, jnp.max(st, axis=0, keepdims=True))
            alpha = jnp.exp2(m_prev - m_new)
            pt = jnp.exp2(st - m_new).astype(BF16)
            m_scr[h, :, cs] = m_new
            vth = vt_ref[h * V_ROWS:(h + 1) * V_ROWS, pl.ds(k0, tk)]
            acc_scr[h, :, cs] = alpha * acc_scr[h, :, cs] + jnp.dot(vth, pt, preferred_element_type=F32)

    def body(j, carry):
        chunk(pl.multiple_of(j * tk, tk), [0] * nsub)
        return carry

    lax.fori_loop(0, qi * (tq // tk), body, 0)
    for d in range(tq // tk):
        kinds = []
        for c in range(nsub):
            if (c + 1) * Q_SUB <= d * tk:
                kinds.append(None)
            elif (d + 1) * tk <= c * Q_SUB:
                kinds.append(0)
            else:
                kinds.append((d * tk,))
        chunk(pl.multiple_of(qi * tq + d * tk, tk), kinds)

    outs = []
    for h in range(2):
        a = acc_scr[h]
        outs.append(a[0:MLA_V, :] / a[MLA_V:MLA_V + 1, :])
    o_ref[...] = jnp.concatenate(outs, axis=0).T.astype(o_ref.dtype)


def _attention(q, k, vt, B, S, *, unit, name, tq=1024, tk=512):
    npair = MLA_HEADS // 2
    return pl.pallas_call(
        functools.partial(_attn_kernel, tq=tq, tk=tk, unit=unit),
        grid=(B, npair, S // tq),
        in_specs=[
            pl.BlockSpec((None, tq, 2 * HEAD_SLOT), lambda b, p, i: (b, i, p)),
            pl.BlockSpec((None, S, 2 * HEAD_SLOT), lambda b, p, i: (b, 0, p)),
            pl.BlockSpec((None, 2 * V_ROWS, S), lambda b, p, i: (b, p, 0)),
        ],
        out_specs=pl.BlockSpec((None, tq, 2 * MLA_V), lambda b, p, i: (b, i, p)),
        out_shape=jax.ShapeDtypeStruct((B, S, npair * 2 * MLA_V), BF16),
        scratch_shapes=[pltpu.VMEM((2, 1, tq), F32), pltpu.VMEM((2, V_ROWS, tq), F32)],
        compiler_params=_cparams(("parallel", "parallel", "arbitrary")),
        name=name,
    )(q, k, vt)


def _merge_kernel(ya_ref, ob_ref, oc_ref, g_ref, x_ref, wa_ref, wb_ref, wc_ref, wo_ref, o_ref):
    D = x_ref.shape[-1]
    ya = jnp.dot(ya_ref[...], wa_ref[...], preferred_element_type=F32)
    yb = jnp.dot(ob_ref[...], wb_ref[...], preferred_element_type=F32)
    yc = jnp.dot(oc_ref[...], wc_ref[...], preferred_element_type=F32)
    merged = g_ref[:, 0:D] * ya + g_ref[:, D:2 * D] * yb + g_ref[:, 2 * D:3 * D] * yc
    o_ref[...] = x_ref[...] + jnp.dot(merged.astype(BF16), wo_ref[...], preferred_element_type=F32)


def _merge(ya, ob, oc, g, x, wa, wb, wc, wo, tm=256):
    T, D = x.shape
    W = ya.shape[-1]
    rows = lambda n: pl.BlockSpec((tm, n), lambda i: (i, 0))
    full = lambda shape: pl.BlockSpec(shape, lambda i: (0, 0))
    return pl.pallas_call(
        _merge_kernel,
        grid=(T // tm,),
        in_specs=[rows(W), rows(W), rows(W), rows(3 * D), rows(D),
                  full((W, D)), full((W, D)), full((W, D)), full((D, D))],
        out_specs=rows(D),
        out_shape=jax.ShapeDtypeStruct((T, D), F32),
        compiler_params=_cparams(("parallel",)),
        name="merge",
    )(ya, ob, oc, g, x, wa, wb, wc, wo)


def _ffn_kernel(x_ref, g_ref, wg_ref, wu_ref, wd_ref, o_ref, h_scr, acc_scr):
    j = pl.program_id(1)
    last = pl.num_programs(1) - 1

    @pl.when(j == 0)
    def _():
        h_scr[...] = _rms(x_ref[...], g_ref[...]).astype(BF16)
        acc_scr[...] = jnp.zeros(acc_scr.shape, F32)

    h = h_scr[...]
    gate = jnp.dot(h, wg_ref[...], preferred_element_type=F32)
    up = jnp.dot(h, wu_ref[...], preferred_element_type=F32)
    act = (jax.nn.silu(gate) * up).astype(BF16)
    acc_scr[...] += jnp.dot(act, wd_ref[...], preferred_element_type=F32)

    @pl.when(j == last)
    def _():
        o_ref[...] = x_ref[...] + acc_scr[...]


def _ffn(x, g, w_gate_up, w_down, tm=512, nf=2):
    T, D = x.shape
    d_ff = w_down.shape[0]
    tf = d_ff // nf
    return pl.pallas_call(
        _ffn_kernel,
        grid=(T // tm, nf),
        in_specs=[
            pl.BlockSpec((tm, D), lambda i, j: (i, 0)),
            pl.BlockSpec((1, D), lambda i, j: (0, 0)),
            pl.BlockSpec((D, tf), lambda i, j: (0, j)),
            pl.BlockSpec((D, tf), lambda i, j: (0, nf + j)),
            pl.BlockSpec((tf, D), lambda i, j: (j, 0)),
        ],
        out_specs=pl.BlockSpec((tm, D), lambda i, j: (i, 0)),
        out_shape=jax.ShapeDtypeStruct((T, D), F32),
        scratch_shapes=[pltpu.VMEM((tm, D), BF16), pltpu.VMEM((tm, D), F32)],
        compiler_params=_cparams(("parallel", "arbitrary")),
        name="swiglu",
    )(x, g.reshape(1, D), w_gate_up, w_gate_up, w_down)


def _ple_kernel(x_ref, p_ref, g_ref, wpg_ref, wple_ref, fn_ref, o_ref, *, final):
    x = x_ref[...]
    hn = _rms(x, g_ref[...]).astype(BF16)
    pg = jax.nn.sigmoid(jnp.dot(hn, wpg_ref[...], preferred_element_type=F32))
    pe = jnp.dot(p_ref[...].astype(BF16), wple_ref[...], preferred_element_type=F32)
    xn = x + pg * pe
    if final:
        xn = _rms(xn, fn_ref[...])
    o_ref[...] = xn


def _ple(x, p, g, wpg, wple, fn, final, tm=512):
    T, D = x.shape
    P = p.shape[-1]
    full = lambda shape: pl.BlockSpec(shape, lambda i: (0, 0))
    return pl.pallas_call(
        functools.partial(_ple_kernel, final=final),
        grid=(T // tm,),
        in_specs=[pl.BlockSpec((tm, D), lambda i: (i, 0)), pl.BlockSpec((tm, P), lambda i: (i, 0)),
                  full((1, D)), full((D, D)), full((P, D)), full((1, D))],
        out_specs=pl.BlockSpec((tm, D), lambda i: (i, 0)),
        out_shape=jax.ShapeDtypeStruct((T, D), F32),
        compiler_params=_cparams(("parallel",)),
        name="ple_final" if final else "ple",
    )(x, p, g.reshape(1, D), wpg, wple, fn.reshape(1, D))


def _block_diag(w):
    H, I, J = w.shape
    eye = jnp.eye(H, dtype=w.dtype)
    return (eye[:, None, :, None] * w[:, :, None, :]).reshape(H * I, H * J)


def _head_slots(w, width):
    K = w.shape[0]
    w = w.reshape(K, MLA_HEADS, width)
    return jnp.pad(w, ((0, 0), (0, 0), (0, HEAD_SLOT - width))).reshape(K, MLA_HEADS * HEAD_SLOT)


def _rope_tables(S):
    pos = jnp.arange(S, dtype=F32)
    inv_freq = ROPE_BASE ** (-jnp.arange(0, MLA_ROPE, 2, dtype=F32) / MLA_ROPE)
    ang = pos[:, None] * inv_freq[None, :]
    cos, sin = jnp.cos(ang), jnp.sin(ang)
    half = MLA_ROPE // 2
    z = lambda n: jnp.zeros((S, n), F32)
    tail = LANES - ROPE_LANE0 - MLA_ROPE
    cq_t = jnp.concatenate([jnp.ones((S, ROPE_LANE0), F32), cos, cos, z(tail)], axis=1)
    ck_t = jnp.concatenate([z(ROPE_LANE0), cos, cos, z(tail)], axis=1)
    s1 = jnp.concatenate([z(ROPE_LANE0), -sin, z(half), z(tail)], axis=1)
    s2 = jnp.concatenate([z(ROPE_LANE0), z(half), sin, z(tail)], axis=1)
    return cq_t, ck_t, s1, s2


def kernel(x, p, mix_norm, w_in, gate_b, conv_w, conv_b, lru_wa, lru_ba, lru_wx, lru_bx, lru_lambda,
           mla_q_norm, mla_wuq, mla_kv_norm, mla_wukv, fox_bf, w_br_a, w_br_b, w_br_c, w_o,
           ffn_norm, w_gate_up, w_down, ple_norm, w_ple_gate, w_ple, final_norm):
    B, S, D = x.shape
    depth = w_in.shape[0]
    T = B * S
    tabs = _rope_tables(S)
    mla_scale = (MLA_NOPE + MLA_ROPE) ** -0.5 * LOG2E
    fox_scale = FOX_HEAD_DIM ** -0.5 * LOG2E

    o_u = 0
    o_cq = 2 * LRU_WIDTH
    o_ckv = o_cq + MLA_Q_LORA
    o_kr = o_ckv + MLA_KV_LORA
    o_fq = o_kr + MLA_ROPE
    o_fl = o_fq + 3 * FOX_WIDTH
    o_g = o_fl + FOX_HEADS

    xf = x.reshape(T, D)
    for i in range(depth):
        wi = w_in[i]
        w_u = wi[:, o_u:o_cq].astype(BF16)
        misc_cols = jnp.concatenate([
            jnp.zeros((D, ROPE_LANE0), F32), wi[:, o_kr:o_fq], wi[:, o_fl:o_g],
            jnp.zeros((D, LANES - ROPE_LANE0 - MLA_ROPE - FOX_HEADS), F32)], axis=1)
        w_lat = jnp.concatenate([wi[:, o_cq:o_kr], misc_cols], axis=1).astype(BF16)
        w_fox = jnp.concatenate([_head_slots(wi[:, o_fq:o_fq + FOX_WIDTH], FOX_HEAD_DIM),
                                 _head_slots(wi[:, o_fq + FOX_WIDTH:o_fq + 2 * FOX_WIDTH], FOX_HEAD_DIM),
                                 wi[:, o_fq + 2 * FOX_WIDTH:o_fl]], axis=1).astype(BF16)
        w_g = wi[:, o_g:].astype(BF16)
        nslot = FOX_HEADS * HEAD_SLOT
        fox_cs = jnp.concatenate([jnp.full((nslot,), fox_scale, F32), jnp.ones((nslot + FOX_WIDTH,), F32)])

        u = _norm_matmul(xf, mix_norm[i], w_u, name="in_lru", out_dtype=F32, tn=512)
        lat = _norm_matmul(xf, mix_norm[i], w_lat, name="in_latent", out_dtype=F32, tn=384)
        fox = _norm_matmul(xf, mix_norm[i], w_fox, name="in_fox", out_dtype=BF16, tn=512, col_scale=fox_cs)
        g = _norm_matmul(xf, mix_norm[i], w_g, name="in_gates", out_dtype=F32, tn=512, col_bias=gate_b[i],
                         act="sigmoid")

        ya = _lru_branch(u, conv_w[i], conv_b[i], _block_diag(lru_wa[i]).astype(BF16), lru_ba[i],
                         _block_diag(lru_wx[i]).astype(BF16), lru_bx[i], lru_lambda[i], B, S)

        wq = _head_slots(mla_wuq[i], MLA_NOPE + MLA_ROPE).astype(BF16)
        wukv = mla_wukv[i].reshape(MLA_KV_LORA, MLA_HEADS, MLA_NOPE + MLA_V)
        wk = _head_slots(wukv[:, :, :MLA_NOPE].reshape(MLA_KV_LORA, MLA_HEADS * MLA_NOPE), MLA_NOPE).astype(BF16)
        wv = wukv[:, :, MLA_NOPE:].reshape(MLA_KV_LORA, MLA_HEADS * MLA_V).astype(BF16)
        bf_row = jnp.zeros((1, LANES), F32).at[0, FLOGIT_LANE0:FLOGIT_LANE0 + FOX_HEADS].set(fox_bf[i])
        q_m, k_m, vt_m, q_f, k_f, vt_f = _prep(lat, fox, tabs, mla_q_norm[i], mla_kv_norm[i], wq, wk, wv, bf_row,
                                               B, S, mla_scale)

        ob = _attention(q_m, k_m, vt_m, B, S, unit=MLA_CHUNK, name="mla_attention")
        oc = _attention(q_f, k_f, vt_f, B, S, unit=1, name="fox_attention")

        xf = _merge(ya.reshape(T, -1), ob.reshape(T, -1), oc.reshape(T, -1), g, xf,
                    w_br_a[i].astype(BF16), w_br_b[i].astype(BF16), w_br_c[i].astype(BF16), w_o[i].astype(BF16))
        xf = _ffn(xf, ffn_norm[i], w_gate_up[i].astype(BF16), w_down[i].astype(BF16))
        xf = _ple(xf, p[i].reshape(T, -1), ple_norm[i], w_ple_gate[i].astype(BF16), w_ple[i].astype(BF16),
                  final_norm, final=(i == depth - 1))
    return xf.reshape(B, S, D)
```

```python
import functools
import math

import jax
import jax.numpy as jnp
from jax import lax
from jax.experimental import pallas as pl
from jax.experimental.pallas import tpu as pltpu

F32 = jnp.float32
BF16 = jnp.bfloat16

EPS = 1e-6
NEG_INF = -1e30
LOG2E = math.log2(math.e)

LRU_WIDTH = 512
LRU_HEADS = 8
CONV_WIDTH = 4
LRU_C = 8.0

MLA_HEADS = 8
MLA_Q_LORA = 384
MLA_KV_LORA = 256
MLA_NOPE = 64
MLA_ROPE = 32
MLA_V = 64
ROPE_BASE = 10000.0
MLA_CHUNK = 64

FOX_HEADS = 8
FOX_HEAD_DIM = 64
FOX_WIDTH = FOX_HEADS * FOX_HEAD_DIM

LANES = 128
SUBLANES = 8
HEAD_SLOT = LANES
ROPE_LANE0 = MLA_NOPE
FLOGIT_LANE0 = MLA_NOPE + MLA_ROPE
V_ROWS = 80
Q_SUB = 256

VMEM_LIMIT = 56 * 1024 * 1024


def _cparams(sem):
    return pltpu.CompilerParams(dimension_semantics=sem, vmem_limit_bytes=VMEM_LIMIT)


def _rms(xf, g):
    ms = jnp.mean(xf * xf, axis=-1, keepdims=True)
    return xf * lax.rsqrt(ms + EPS) * g


def _sigmoid(z):
    return 0.5 * jnp.tanh(0.5 * z) + 0.5


def _softplus(z):
    return jnp.maximum(z, 0.0) + jnp.log1p(jnp.exp(-jnp.abs(z)))


IN_TN = 512


def _in_proj_kernel(x_ref, g_ref, w_ref, cs_ref, cb_ref, u_ref, lat_ref, fox_ref, gate_ref, h_scr, *, bounds):
    j = pl.program_id(1)

    @pl.when(j == 0)
    def _():
        h_scr[...] = _rms(x_ref[...], g_ref[...]).astype(BF16)

    acc = jnp.dot(h_scr[...], w_ref[...], preferred_element_type=F32)
    b_u, b_lat, b_fox = bounds

    @pl.when(j < b_u)
    def _():
        u_ref[...] = acc.astype(u_ref.dtype)

    @pl.when((j >= b_u) & (j < b_lat))
    def _():
        lat_ref[...] = acc

    @pl.when((j >= b_lat) & (j < b_fox))
    def _():
        fox_ref[...] = (acc * cs_ref[...]).astype(fox_ref.dtype)

    @pl.when(j >= b_fox)
    def _():
        gate_ref[...] = _sigmoid(acc + cb_ref[...]).astype(gate_ref.dtype)


def _in_proj(x, g, w_all, cs, cb, widths, tm=1024):
    T, D = x.shape
    tn = IN_TN
    n_u, n_lat, n_fox, n_gate = (w // tn for w in widths)
    bounds = (n_u, n_u + n_lat, n_u + n_lat + n_fox)
    ntile = bounds[2] + n_gate

    def slab(start, count):
        return pl.BlockSpec((tm, tn), lambda i, j: (i, jnp.clip(j - start, 0, count - 1)))

    return pl.pallas_call(
        functools.partial(_in_proj_kernel, bounds=bounds),
        grid=(T // tm, ntile),
        in_specs=[
            pl.BlockSpec((tm, D), lambda i, j: (i, 0)),
            pl.BlockSpec((1, D), lambda i, j: (0, 0)),
            pl.BlockSpec((D, tn), lambda i, j: (0, j)),
            pl.BlockSpec((1, tn), lambda i, j: (0, j)),
            pl.BlockSpec((1, tn), lambda i, j: (0, j)),
        ],
        out_specs=[slab(0, n_u), slab(bounds[0], n_lat), slab(bounds[1], n_fox), slab(bounds[2], n_gate)],
        out_shape=[jax.ShapeDtypeStruct((T, widths[0]), BF16), jax.ShapeDtypeStruct((T, widths[1]), F32),
                   jax.ShapeDtypeStruct((T, widths[2]), BF16), jax.ShapeDtypeStruct((T, widths[3]), BF16)],
        scratch_shapes=[pltpu.VMEM((tm, D), BF16)],
        compiler_params=_cparams(("parallel", "arbitrary")),
        name="in_proj",
    )(x, g.reshape(1, D), w_all, cs, cb)


def _lru_kernel(u_ref, cw_ref, cb_ref, wa_ref, ba_ref, wx_ref, bx_ref, lam_ref, o_ref, ext_scr, hc_scr, a_scr,
                b_scr, *, ts):
    W = LRU_WIDTH
    pad = 8

    @pl.when(pl.program_id(1) == 0)
    def _():
        ext_scr[0:pad, :] = jnp.zeros((pad, W), F32)
        hc_scr[...] = jnp.zeros((1, W), F32)

    u = u_ref[:, 0:W].astype(F32)
    ug = u_ref[:, W:2 * W].astype(F32)
    ext_scr[pad:pad + ts, :] = u
    xc = cb_ref[...] + ext_scr[pl.ds(pad - 3, ts), :] * cw_ref[0:1, :]
    xc = xc + ext_scr[pl.ds(pad - 2, ts), :] * cw_ref[1:2, :]
    xc = xc + ext_scr[pl.ds(pad - 1, ts), :] * cw_ref[2:3, :]
    xc = xc + u * cw_ref[3:4, :]
    ext_scr[0:pad, :] = u[ts - pad:ts, :]

    xb = xc.astype(BF16)
    r = _sigmoid(jnp.dot(xb, wa_ref[...], preferred_element_type=F32) + ba_ref[...])
    ig = _sigmoid(jnp.dot(xb, wx_ref[...], preferred_element_type=F32) + bx_ref[...])
    log_a = (-LRU_C) * r * _softplus(-lam_ref[...])
    a = jnp.exp(log_a)
    y2 = 2.0 * log_a
    b = jnp.sqrt(jnp.tanh(-0.5 * y2) * (jnp.exp(y2) + 1.0)) * (ig * xc)

    a_scr[...] = a
    b_scr[...] = b
    row = lax.broadcasted_iota(jnp.int32, (SUBLANES, W), 0)

    def block(i, h_prev):
        r0 = pl.multiple_of(i * SUBLANES, SUBLANES)
        aa = a_scr[pl.ds(r0, SUBLANES), :]
        bb = b_scr[pl.ds(r0, SUBLANES), :]
        d = 1
        while d < SUBLANES:
            keep = row >= d
            a_sh = jnp.where(keep, pltpu.roll(aa, d, 0), 1.0)
            b_sh = jnp.where(keep, pltpu.roll(bb, d, 0), 0.0)
            bb = aa * b_sh + bb
            aa = aa * a_sh
            d *= 2
        h = bb + aa * h_prev
        b_scr[pl.ds(r0, SUBLANES), :] = h
        return h[SUBLANES - 1:SUBLANES, :]

    hc_scr[...] = lax.fori_loop(0, ts // SUBLANES, block, hc_scr[...], unroll=4)
    o_ref[...] = (b_scr[...] * jax.nn.gelu(ug)).astype(o_ref.dtype)


def _lru_branch(u, conv_w, conv_b, wa_bd, ba, wx_bd, bx, lam, B, S, ts=256):
    W = LRU_WIDTH
    row = lambda v: v.reshape(1, W)
    full = lambda shape: pl.BlockSpec(shape, lambda b, t: (0,) * len(shape))
    return pl.pallas_call(
        functools.partial(_lru_kernel, ts=ts),
        grid=(B, S // ts),
        in_specs=[
            pl.BlockSpec((None, ts, 2 * W), lambda b, t: (b, t, 0)),
            full((CONV_WIDTH, W)), full((1, W)),
            full((W, W)), full((1, W)), full((W, W)), full((1, W)), full((1, W)),
        ],
        out_specs=pl.BlockSpec((None, ts, W), lambda b, t: (b, t, 0)),
        out_shape=jax.ShapeDtypeStruct((B, S, W), BF16),
        scratch_shapes=[pltpu.VMEM((ts + 8, W), F32), pltpu.VMEM((1, W), F32), pltpu.VMEM((ts, W), F32),
                        pltpu.VMEM((ts, W), F32)],
        compiler_params=_cparams(("parallel", "arbitrary")),
        name="lru_scan",
    )(u.reshape(B, S, 2 * W), conv_w, row(conv_b), wa_bd, row(ba), wx_bd, row(bx), row(lam))


def _rope_slot(x, c, s1, s2):
    return x * c + pltpu.roll(x, LANES - MLA_ROPE // 2, 1) * s1 + pltpu.roll(x, MLA_ROPE // 2, 1) * s2


def _bf16_terms(c):
    hi = c.astype(BF16).astype(F32)
    r = c - hi
    lo = r.astype(BF16).astype(F32)
    return hi, lo, (r - lo).astype(BF16).astype(F32)


def _fox_bias_tables():
    rows = jnp.arange(3 * LANES)
    term, head = rows // LANES, rows % LANES - FLOGIT_LANE0
    valid = (head >= 0) & (head < FOX_HEADS)
    cols = jnp.arange(FOX_HEADS * HEAD_SLOT)
    col_head, col_lane = cols // HEAD_SLOT, cols % HEAD_SLOT - FOX_HEAD_DIM
    hit = valid[:, None] & (head[:, None] == col_head[None, :])
    sel_q = jnp.where(hit & (col_lane[None, :] == term[:, None]), 1.0, 0.0).astype(BF16)
    sel_k = jnp.where(hit & (col_lane[None, :] == term[:, None] + 3), -1.0, 0.0).astype(BF16)
    one_q = jnp.where((col_lane >= 3) & (col_lane < 6), 1.0, 0.0).astype(F32)[None, :]
    one_k = jnp.where((col_lane >= 0) & (col_lane < 3), 1.0, 0.0).astype(F32)[None, :]
    return sel_q, sel_k, one_q, one_k


def _store_values_t(vt_ref, v, ts):
    vt = v.T
    row = lax.broadcasted_iota(jnp.int32, (V_ROWS - MLA_V, ts), 0)
    ones_tile = jnp.where(row == 0, 1.0, 0.0).astype(vt_ref.dtype)
    for h in range(MLA_HEADS):
        vt_ref[h * V_ROWS:h * V_ROWS + MLA_V, :] = vt[h * MLA_V:(h + 1) * MLA_V, :].astype(vt_ref.dtype)
        vt_ref[h * V_ROWS + MLA_V:(h + 1) * V_ROWS, :] = ones_tile


def _prep_kernel(lat_ref, fox_ref, cq_t_ref, ck_t_ref, s1_ref, s2_ref, qn_ref, kvn_ref, wq_ref, wk_ref, wv_ref,
                 bf_ref, selq_ref, selk_ref, oneq_ref, onek_ref, qm_ref, km_ref, vtm_ref, qf_ref, kf_ref, vtf_ref,
                 carry_scr, *, ts, q_scale):
    @pl.when(pl.program_id(1) == 0)
    def _():
        carry_scr[...] = jnp.zeros((1, LANES), F32)

    cq_t, ck_t, s1, s2 = cq_t_ref[...], ck_t_ref[...], s1_ref[...], s2_ref[...]

    cqn = _rms(lat_ref[:, 0:MLA_Q_LORA], qn_ref[...]).astype(BF16)
    q = jnp.dot(cqn, wq_ref[...], preferred_element_type=F32)
    for h in range(MLA_HEADS):
        sl = slice(h * HEAD_SLOT, (h + 1) * HEAD_SLOT)
        qm_ref[:, sl] = (_rope_slot(q[:, sl], cq_t, s1, s2) * q_scale).astype(qm_ref.dtype)

    ckvn = _rms(lat_ref[:, MLA_Q_LORA:MLA_Q_LORA + MLA_KV_LORA], kvn_ref[...]).astype(BF16)
    kn = jnp.dot(ckvn, wk_ref[...], preferred_element_type=F32)
    misc = lat_ref[:, MLA_Q_LORA + MLA_KV_LORA:MLA_Q_LORA + MLA_KV_LORA + LANES]
    kr = _rope_slot(misc, ck_t, s1, s2)
    for h in range(MLA_HEADS):
        sl = slice(h * HEAD_SLOT, (h + 1) * HEAD_SLOT)
        km_ref[:, sl] = (kn[:, sl] + kr).astype(km_ref.dtype)
    _store_values_t(vtm_ref, jnp.dot(ckvn, wv_ref[...], preferred_element_type=F32), ts)

    c = -_softplus(-(misc + bf_ref[...]))
    row = lax.broadcasted_iota(jnp.int32, (ts, LANES), 0)
    d = 1
    while d < ts:
        c = c + jnp.where(row >= d, pltpu.roll(c, d, 0), 0.0)
        d *= 2
    c = c + carry_scr[...]
    carry_scr[...] = c[ts - 1:ts, :]
    c = c * LOG2E

    terms = jnp.concatenate(_bf16_terms(c), axis=1).astype(BF16)
    nslot = FOX_HEADS * HEAD_SLOT
    q_bias = jnp.dot(terms, selq_ref[...], preferred_element_type=F32) + oneq_ref[...]
    k_bias = jnp.dot(terms, selk_ref[...], preferred_element_type=F32) + onek_ref[...]
    qf_ref[...] = (fox_ref[:, 0:nslot].astype(F32) + q_bias).astype(qf_ref.dtype)
    kf_ref[...] = (fox_ref[:, nslot:2 * nslot].astype(F32) + k_bias).astype(kf_ref.dtype)
    _store_values_t(vtf_ref, fox_ref[:, 2 * nslot:2 * nslot + FOX_WIDTH].astype(F32), ts)


def _prep(lat, fox, tabs, qn, kvn, wq, wk, wv, bf_row, bias_tabs, B, S, q_scale, ts=512):
    NL, NF = lat.shape[-1], fox.shape[-1]
    full = lambda shape: pl.BlockSpec(shape, lambda b, t: (0,) * len(shape))
    tab = pl.BlockSpec((ts, LANES), lambda b, t: (t, 0))
    HS = MLA_HEADS * HEAD_SLOT
    slots = pl.BlockSpec((None, ts, HS), lambda b, t: (b, t, 0))
    vts = pl.BlockSpec((None, MLA_HEADS * V_ROWS, ts), lambda b, t: (b, 0, t))
    slot_shape = jax.ShapeDtypeStruct((B, S, HS), BF16)
    vt_shape = jax.ShapeDtypeStruct((B, MLA_HEADS * V_ROWS, S), BF16)
    return pl.pallas_call(
        functools.partial(_prep_kernel, ts=ts, q_scale=q_scale),
        grid=(B, S // ts),
        in_specs=[
            pl.BlockSpec((None, ts, NL), lambda b, t: (b, t, 0)),
            pl.BlockSpec((None, ts, NF), lambda b, t: (b, t, 0)),
            tab, tab, tab, tab,
            full((1, MLA_Q_LORA)), full((1, MLA_KV_LORA)),
            full((MLA_Q_LORA, HS)), full((MLA_KV_LORA, HS)), full((MLA_KV_LORA, MLA_HEADS * MLA_V)),
            full((1, LANES)),
            full((3 * LANES, HS)), full((3 * LANES, HS)), full((1, HS)), full((1, HS)),
        ],
        out_specs=[slots, slots, vts, slots, slots, vts],
        out_shape=[slot_shape, slot_shape, vt_shape, slot_shape, slot_shape, vt_shape],
        scratch_shapes=[pltpu.VMEM((1, LANES), F32)],
        compiler_params=_cparams(("parallel", "arbitrary")),
        name="mla_fox_prep",
    )(lat.reshape(B, S, NL), fox.reshape(B, S, NF), *tabs, qn.reshape(1, -1), kvn.reshape(1, -1), wq, wk, wv,
      bf_row, *bias_tabs)


def _attn_kernel(q_ref, k_ref, vt_ref, o_ref, m_scr, acc_scr, sa_scr, sb_scr, *, tq, tk, unit):
    qi = pl.program_id(2)
    nsub = tq // Q_SUB
    ndiag = tq // tk
    m_scr[...] = jnp.full(m_scr.shape, NEG_INF, F32)
    acc_scr[...] = jnp.zeros(acc_scr.shape, F32)
    all_visible = [0] * nsub

    def diag_kinds(d):
        kinds = []
        for c in range(nsub):
            if (c + 1) * Q_SUB <= d * tk:
                kinds.append(None)
            elif (d + 1) * tk <= c * Q_SUB:
                kinds.append(0)
            else:
                kinds.append((d * tk,))
        return kinds

    def streams(kinds):
        return [(h, c) for c in range(nsub) for h in range(2) if kinds[c] is not None]

    def logits_one(s_scr, k0, h, c):
        kh = k_ref[pl.ds(k0, tk), h * HEAD_SLOT:(h + 1) * HEAD_SLOT]
        qh = q_ref[c * Q_SUB:(c + 1) * Q_SUB, h * HEAD_SLOT:(h + 1) * HEAD_SLOT]
        s_scr[2 * c + h] = lax.dot_general(kh, qh, (((1,), (1,)), ((), ())), preferred_element_type=F32)

    def softmax_one(s_scr, k0, kind, h, c):
        st = s_scr[2 * c + h]
        cs = slice(c * Q_SUB, (c + 1) * Q_SUB)
        if kind != 0:
            kk = lax.broadcasted_iota(jnp.int32, (tk, Q_SUB), 0) + kind[0]
            qq = lax.broadcasted_iota(jnp.int32, (tk, Q_SUB), 1) + c * Q_SUB
            if unit > 1:
                sh = unit.bit_length() - 1
                kk, qq = kk >> sh, qq >> sh
            st = jnp.where(qq >= kk, st, NEG_INF)
        m_prev = m_scr[h, :, cs]
        m_new = jnp.maximum(m_prev, jnp.max(st, axis=0, keepdims=True))
        alpha = jnp.exp2(m_prev - m_new)
        pt = jnp.exp2(st - m_new).astype(BF16)
        m_scr[h, :, cs] = m_new
        vth = vt_ref[h * V_ROWS:(h + 1) * V_ROWS, pl.ds(k0, tk)]
        acc_scr[h, :, cs] = alpha * acc_scr[h, :, cs] + jnp.dot(vth, pt, preferred_element_type=F32)

    def stage(nxt, k_nxt, kinds_nxt, cur, k_cur, kinds_cur):
        sn = streams(kinds_nxt) if nxt is not None else []
        sc = streams(kinds_cur) if cur is not None else []
        for idx in range(max(len(sn), len(sc))):
            if idx < len(sn):
                logits_one(nxt, k_nxt, *sn[idx])
            if idx < len(sc):
                h, c = sc[idx]
                softmax_one(cur, k_cur, kinds_cur[c], h, c)

    n = qi * ndiag
    kd = [pl.multiple_of(qi * tq + d * tk, tk) for d in range(ndiag)]
    bufs = [sa_scr, sb_scr]
    stage(bufs[0], kd[0], diag_kinds(0), None, None, None)
    for d in range(ndiag):
        if d + 1 < ndiag:
            stage(bufs[(d + 1) % 2], kd[d + 1], diag_kinds(d + 1), bufs[d % 2], kd[d], diag_kinds(d))
        else:
            stage(bufs[(d + 1) % 2], 0, all_visible, bufs[d % 2], kd[d], diag_kinds(d))
    first = bufs[ndiag % 2]
    second = bufs[(ndiag + 1) % 2]

    def body(i, carry):
        k_a = pl.multiple_of(2 * i * tk, tk)
        k_b = pl.multiple_of((2 * i + 1) * tk, tk)
        k_c = pl.multiple_of(jnp.minimum(2 * i + 2, n - 1) * tk, tk)
        stage(second, k_b, all_visible, first, k_a, all_visible)
        stage(first, k_c, all_visible, second, k_b, all_visible)
        return carry

    lax.fori_loop(0, n // 2, body, 0)

    outs = []
    for h in range(2):
        a = acc_scr[h]
        outs.append(a[0:MLA_V, :] / a[MLA_V:MLA_V + 1, :])
    o_ref[...] = jnp.concatenate(outs, axis=0).T.astype(o_ref.dtype)


def _attention(q, k, vt, B, S, *, unit, name, tq=1024, tk=512):
    npair = MLA_HEADS // 2
    nstream = 2 * (tq // Q_SUB)
    assert (tq // tk) % 2 == 0
    return pl.pallas_call(
        functools.partial(_attn_kernel, tq=tq, tk=tk, unit=unit),
        grid=(B, npair, S // tq),
        in_specs=[
            pl.BlockSpec((None, tq, 2 * HEAD_SLOT), lambda b, p, i: (b, i, p)),
            pl.BlockSpec((None, S, 2 * HEAD_SLOT), lambda b, p, i: (b, 0, p)),
            pl.BlockSpec((None, 2 * V_ROWS, S), lambda b, p, i: (b, p, 0)),
        ],
        out_specs=pl.BlockSpec((None, tq, 2 * MLA_V), lambda b, p, i: (b, i, p)),
        out_shape=jax.ShapeDtypeStruct((B, S, npair * 2 * MLA_V), BF16),
        scratch_shapes=[pltpu.VMEM((2, 1, tq), F32), pltpu.VMEM((2, V_ROWS, tq), F32),
                        pltpu.VMEM((nstream, tk, Q_SUB), F32), pltpu.VMEM((nstream, tk, Q_SUB), F32)],
        compiler_params=_cparams(("parallel", "parallel", "arbitrary")),
        name=name,
    )(q, k, vt)


def _merge_kernel(ya_ref, ob_ref, oc_ref, g_ref, x_ref, wa_ref, wb_ref, wc_ref, wo_ref, o_ref):
    D = x_ref.shape[-1]
    ya = jnp.dot(ya_ref[...], wa_ref[...], preferred_element_type=F32)
    yb = jnp.dot(ob_ref[...], wb_ref[...], preferred_element_type=F32)
    yc = jnp.dot(oc_ref[...], wc_ref[...], preferred_element_type=F32)
    merged = g_ref[:, 0:D] * ya + g_ref[:, D:2 * D] * yb + g_ref[:, 2 * D:3 * D] * yc
    o_ref[...] = x_ref[...] + jnp.dot(merged.astype(BF16), wo_ref[...], preferred_element_type=F32)


def _merge(ya, ob, oc, g, x, wa, wb, wc, wo, tm=256):
    T, D = x.shape
    W = ya.shape[-1]
    rows = lambda n: pl.BlockSpec((tm, n), lambda i: (i, 0))
    full = lambda shape: pl.BlockSpec(shape, lambda i: (0, 0))
    return pl.pallas_call(
        _merge_kernel,
        grid=(T // tm,),
        in_specs=[rows(W), rows(W), rows(W), rows(3 * D), rows(D),
                  full((W, D)), full((W, D)), full((W, D)), full((D, D))],
        out_specs=rows(D),
        out_shape=jax.ShapeDtypeStruct((T, D), F32),
        compiler_params=_cparams(("parallel",)),
        name="merge",
    )(ya, ob, oc, g, x, wa, wb, wc, wo)


def _ffn_kernel(x_ref, g_ref, wg_ref, wu_ref, wd_ref, o_ref, h_scr, acc_scr):
    j = pl.program_id(1)
    last = pl.num_programs(1) - 1

    @pl.when(j == 0)
    def _():
        h_scr[...] = _rms(x_ref[...], g_ref[...]).astype(BF16)
        acc_scr[...] = jnp.zeros(acc_scr.shape, F32)

    h = h_scr[...]
    gate = jnp.dot(h, wg_ref[...], preferred_element_type=F32)
    up = jnp.dot(h, wu_ref[...], preferred_element_type=F32)
    act = (gate * _sigmoid(gate) * up).astype(BF16)
    acc_scr[...] += jnp.dot(act, wd_ref[...], preferred_element_type=F32)

    @pl.when(j == last)
    def _():
        o_ref[...] = x_ref[...] + acc_scr[...]


def _ffn(x, g, w_gate_up, w_down, tm=512, nf=2):
    T, D = x.shape
    d_ff = w_down.shape[0]
    tf = d_ff // nf
    return pl.pallas_call(
        _ffn_kernel,
        grid=(T // tm, nf),
        in_specs=[
            pl.BlockSpec((tm, D), lambda i, j: (i, 0)),
            pl.BlockSpec((1, D), lambda i, j: (0, 0)),
            pl.BlockSpec((D, tf), lambda i, j: (0, j)),
            pl.BlockSpec((D, tf), lambda i, j: (0, nf + j)),
            pl.BlockSpec((tf, D), lambda i, j: (j, 0)),
        ],
        out_specs=pl.BlockSpec((tm, D), lambda i, j: (i, 0)),
        out_shape=jax.ShapeDtypeStruct((T, D), F32),
        scratch_shapes=[pltpu.VMEM((tm, D), BF16), pltpu.VMEM((tm, D), F32)],
        compiler_params=_cparams(("parallel", "arbitrary")),
        name="swiglu",
    )(x, g.reshape(1, D), w_gate_up, w_gate_up, w_down)


def _ple_kernel(x_ref, p_ref, g_ref, wpg_ref, wple_ref, fn_ref, o_ref, *, final):
    x = x_ref[...]
    hn = _rms(x, g_ref[...]).astype(BF16)
    pg = _sigmoid(jnp.dot(hn, wpg_ref[...], preferred_element_type=F32))
    pe = jnp.dot(p_ref[...].astype(BF16), wple_ref[...], preferred_element_type=F32)
    xn = x + pg * pe
    if final:
        xn = _rms(xn, fn_ref[...])
    o_ref[...] = xn


def _ple(x, p, g, wpg, wple, fn, final, tm=512):
    T, D = x.shape
    P = p.shape[-1]
    full = lambda shape: pl.BlockSpec(shape, lambda i: (0, 0))
    return pl.pallas_call(
        functools.partial(_ple_kernel, final=final),
        grid=(T // tm,),
        in_specs=[pl.BlockSpec((tm, D), lambda i: (i, 0)), pl.BlockSpec((tm, P), lambda i: (i, 0)),
                  full((1, D)), full((D, D)), full((P, D)), full((1, D))],
        out_specs=pl.BlockSpec((tm, D), lambda i: (i, 0)),
        out_shape=jax.ShapeDtypeStruct((T, D), F32),
        compiler_params=_cparams(("parallel",)),
        name="ple_final" if final else "ple",
    )(x, p, g.reshape(1, D), wpg, wple, fn.reshape(1, D))


def _block_diag(w):
    H, I, J = w.shape
    eye = jnp.eye(H, dtype=w.dtype)
    return (eye[:, None, :, None] * w[:, :, None, :]).reshape(H * I, H * J)


def _head_slots(w, width):
    K = w.shape[0]
    w = w.reshape(K, MLA_HEADS, width)
    return jnp.pad(w, ((0, 0), (0, 0), (0, HEAD_SLOT - width))).reshape(K, MLA_HEADS * HEAD_SLOT)


def _rope_tables(S):
    pos = jnp.arange(S, dtype=F32)
    inv_freq = ROPE_BASE ** (-jnp.arange(0, MLA_ROPE, 2, dtype=F32) / MLA_ROPE)
    ang = pos[:, None] * inv_freq[None, :]
    cos, sin = jnp.cos(ang), jnp.sin(ang)
    half = MLA_ROPE // 2
    z = lambda n: jnp.zeros((S, n), F32)
    tail = LANES - ROPE_LANE0 - MLA_ROPE
    cq_t = jnp.concatenate([jnp.ones((S, ROPE_LANE0), F32), cos, cos, z(tail)], axis=1)
    ck_t = jnp.concatenate([z(ROPE_LANE0), cos, cos, z(tail)], axis=1)
    s1 = jnp.concatenate([z(ROPE_LANE0), -sin, z(half), z(tail)], axis=1)
    s2 = jnp.concatenate([z(ROPE_LANE0), z(half), sin, z(tail)], axis=1)
    return cq_t, ck_t, s1, s2


def kernel(x, p, mix_norm, w_in, gate_b, conv_w, conv_b, lru_wa, lru_ba, lru_wx, lru_bx, lru_lambda,
           mla_q_norm, mla_wuq, mla_kv_norm, mla_wukv, fox_bf, w_br_a, w_br_b, w_br_c, w_o,
           ffn_norm, w_gate_up, w_down, ple_norm, w_ple_gate, w_ple, final_norm):
    B, S, D = x.shape
    depth = w_in.shape[0]
    T = B * S
    tabs = _rope_tables(S)
    bias_tabs = _fox_bias_tables()
    mla_scale = (MLA_NOPE + MLA_ROPE) ** -0.5 * LOG2E
    fox_scale = FOX_HEAD_DIM ** -0.5 * LOG2E

    o_u = 0
    o_cq = 2 * LRU_WIDTH
    o_ckv = o_cq + MLA_Q_LORA
    o_kr = o_ckv + MLA_KV_LORA
    o_fq = o_kr + MLA_ROPE
    o_fl = o_fq + 3 * FOX_WIDTH
    o_g = o_fl + FOX_HEADS

    xf = x.reshape(T, D)
    for i in range(depth):
        wi = w_in[i]
        zcols = lambda n: jnp.zeros((D, n), F32)
        lat_w = MLA_Q_LORA + MLA_KV_LORA + LANES
        lat_pad = -lat_w % IN_TN
        nslot = FOX_HEADS * HEAD_SLOT
        w_all = jnp.concatenate([
            wi[:, o_u:o_cq],
            wi[:, o_cq:o_kr], zcols(ROPE_LANE0), wi[:, o_kr:o_fq], wi[:, o_fl:o_g],
            zcols(LANES - ROPE_LANE0 - MLA_ROPE - FOX_HEADS + lat_pad),
            _head_slots(wi[:, o_fq:o_fq + FOX_WIDTH], FOX_HEAD_DIM),
            _head_slots(wi[:, o_fq + FOX_WIDTH:o_fq + 2 * FOX_WIDTH], FOX_HEAD_DIM),
            wi[:, o_fq + 2 * FOX_WIDTH:o_fl],
            wi[:, o_g:]], axis=1).astype(BF16)
        widths = (2 * LRU_WIDTH, lat_w + lat_pad, 2 * nslot + FOX_WIDTH, 3 * D)
        n_before_fox, n_before_gate = widths[0] + widths[1], widths[0] + widths[1] + widths[2]
        cs = jnp.ones((1, sum(widths)), F32).at[0, n_before_fox:n_before_fox + nslot].set(fox_scale)
        cb = jnp.zeros((1, sum(widths)), F32).at[0, n_before_gate:].set(gate_b[i])
        u, lat, fox, g = _in_proj(xf, mix_norm[i], w_all, cs, cb, widths)

        ya = _lru_branch(u, conv_w[i], conv_b[i], _block_diag(lru_wa[i]).astype(BF16), lru_ba[i],
                         _block_diag(lru_wx[i]).astype(BF16), lru_bx[i], lru_lambda[i], B, S)

        wq = _head_slots(mla_wuq[i], MLA_NOPE + MLA_ROPE).astype(BF16)
        wukv = mla_wukv[i].reshape(MLA_KV_LORA, MLA_HEADS, MLA_NOPE + MLA_V)
        wk = _head_slots(wukv[:, :, :MLA_NOPE].reshape(MLA_KV_LORA, MLA_HEADS * MLA_NOPE), MLA_NOPE).astype(BF16)
        wv = wukv[:, :, MLA_NOPE:].reshape(MLA_KV_LORA, MLA_HEADS * MLA_V).astype(BF16)
        bf_row = jnp.zeros((1, LANES), F32).at[0, FLOGIT_LANE0:FLOGIT_LANE0 + FOX_HEADS].set(fox_bf[i])
        q_m, k_m, vt_m, q_f, k_f, vt_f = _prep(lat, fox, tabs, mla_q_norm[i], mla_kv_norm[i], wq, wk, wv, bf_row,
                                               bias_tabs, B, S, mla_scale)

        ob = _attention(q_m, k_m, vt_m, B, S, unit=MLA_CHUNK, name="mla_attention")
        oc = _attention(q_f, k_f, vt_f, B, S, unit=1, name="fox_attention")

        xf = _merge(ya.reshape(T, -1), ob.reshape(T, -1), oc.reshape(T, -1), g, xf,
                    w_br_a[i].astype(BF16), w_br_b[i].astype(BF16), w_br_c[i].astype(BF16), w_o[i].astype(BF16))
        xf = _ffn(xf, ffn_norm[i], w_gate_up[i].astype(BF16), w_down[i].astype(BF16))
        xf = _ple(xf, p[i].reshape(T, -1), ple_norm[i], w_ple_gate[i].astype(BF16), w_ple[i].astype(BF16),
                  final_norm, final=(i == depth - 1))
    return xf.reshape(B, S, D)
```

```python
import functools
import math

import jax
import jax.numpy as jnp
from jax import lax
from jax.experimental import pallas as pl
from jax.experimental.pallas import tpu as pltpu

F32 = jnp.float32
BF16 = jnp.bfloat16

EPS = 1e-6
NEG_INF = -1e30
LOG2E = math.log2(math.e)

LRU_WIDTH = 512
LRU_HEADS = 8
CONV_WIDTH = 4
LRU_C = 8.0

MLA_HEADS = 8
MLA_Q_LORA = 384
MLA_KV_LORA = 256
MLA_NOPE = 64
MLA_ROPE = 32
MLA_V = 64
ROPE_BASE = 10000.0
MLA_CHUNK = 64

FOX_HEADS = 8
FOX_HEAD_DIM = 64
FOX_WIDTH = FOX_HEADS * FOX_HEAD_DIM

LANES = 128
SUBLANES = 8
HEAD_SLOT = LANES
ROPE_LANE0 = MLA_NOPE
FLOGIT_LANE0 = MLA_NOPE + MLA_ROPE
V_ROWS = 80
Q_SUB = 256

VMEM_LIMIT = 56 * 1024 * 1024


def _cparams(sem):
    return pltpu.CompilerParams(dimension_semantics=sem, vmem_limit_bytes=VMEM_LIMIT)


def _rms(xf, g):
    ms = jnp.mean(xf * xf, axis=-1, keepdims=True)
    return xf * lax.rsqrt(ms + EPS) * g


def _sigmoid(z):
    return 0.5 * jnp.tanh(0.5 * z) + 0.5


def _softplus(z):
    return jnp.maximum(z, 0.0) + jnp.log1p(jnp.exp(-jnp.abs(z)))


IN_TN = 512


def _in_proj_kernel(x_ref, g_ref, w_ref, cs_ref, cb_ref, u_ref, lat_ref, fox_ref, gate_ref, *, widths):
    h = _rms(x_ref[...], g_ref[...]).astype(BF16)
    outs = (u_ref, lat_ref, fox_ref, gate_ref)
    col = 0
    for slab, (o_ref, width) in enumerate(zip(outs, widths)):
        for j in range(width // IN_TN):
            sl = slice(col + j * IN_TN, col + (j + 1) * IN_TN)
            acc = jnp.dot(h, w_ref[:, sl], preferred_element_type=F32)
            if slab == 2:
                acc = acc * cs_ref[:, sl]
            if slab == 3:
                acc = _sigmoid(acc + cb_ref[:, sl])
            o_ref[:, j * IN_TN:(j + 1) * IN_TN] = acc.astype(o_ref.dtype)
        col += width


def _resident(shape):
    return pl.BlockSpec(shape, lambda i: (0, 0), pipeline_mode=pl.Buffered(1))


def _in_proj(x, g, w_all, cs, cb, widths, tm=512):
    T, D = x.shape
    N = sum(widths)
    return pl.pallas_call(
        functools.partial(_in_proj_kernel, widths=widths),
        grid=(T // tm,),
        in_specs=[
            pl.BlockSpec((tm, D), lambda i: (i, 0)),
            _resident((1, D)), _resident((D, N)), _resident((1, N)), _resident((1, N)),
        ],
        out_specs=[pl.BlockSpec((tm, w), lambda i: (i, 0)) for w in widths],
        out_shape=[jax.ShapeDtypeStruct((T, widths[0]), BF16), jax.ShapeDtypeStruct((T, widths[1]), F32),
                   jax.ShapeDtypeStruct((T, widths[2]), BF16), jax.ShapeDtypeStruct((T, widths[3]), BF16)],
        compiler_params=_cparams(("parallel",)),
        name="in_proj",
    )(x, g.reshape(1, D), w_all, cs, cb)


def _lru_kernel(u_ref, cw_ref, cb_ref, wa_ref, ba_ref, wx_ref, bx_ref, lam_ref, o_ref, ext_scr, hc_scr, a_scr,
                b_scr, *, ts):
    W = LRU_WIDTH
    pad = 8

    @pl.when(pl.program_id(1) == 0)
    def _():
        ext_scr[0:pad, :] = jnp.zeros((pad, W), F32)
        hc_scr[...] = jnp.zeros((1, W), F32)

    u = u_ref[:, 0:W].astype(F32)
    ug = u_ref[:, W:2 * W].astype(F32)
    ext_scr[pad:pad + ts, :] = u
    xc = cb_ref[...] + ext_scr[pl.ds(pad - 3, ts), :] * cw_ref[0:1, :]
    xc = xc + ext_scr[pl.ds(pad - 2, ts), :] * cw_ref[1:2, :]
    xc = xc + ext_scr[pl.ds(pad - 1, ts), :] * cw_ref[2:3, :]
    xc = xc + u * cw_ref[3:4, :]
    ext_scr[0:pad, :] = u[ts - pad:ts, :]

    xb = xc.astype(BF16)
    r = _sigmoid(jnp.dot(xb, wa_ref[...], preferred_element_type=F32) + ba_ref[...])
    ig = _sigmoid(jnp.dot(xb, wx_ref[...], preferred_element_type=F32) + bx_ref[...])
    log_a = (-LRU_C) * r * _softplus(-lam_ref[...])
    a = jnp.exp(log_a)
    b = jnp.sqrt(jnp.tanh(-log_a) * (a * a + 1.0)) * (ig * xc)

    a_scr[...] = a
    b_scr[...] = b
    row = lax.broadcasted_iota(jnp.int32, (SUBLANES, W), 0)

    def block(i, h_prev):
        r0 = pl.multiple_of(i * SUBLANES, SUBLANES)
        aa = a_scr[pl.ds(r0, SUBLANES), :]
        bb = b_scr[pl.ds(r0, SUBLANES), :]
        d = 1
        while d < SUBLANES:
            keep = row >= d
            a_sh = jnp.where(keep, pltpu.roll(aa, d, 0), 1.0)
            b_sh = jnp.where(keep, pltpu.roll(bb, d, 0), 0.0)
            bb = aa * b_sh + bb
            aa = aa * a_sh
            d *= 2
        h = bb + aa * h_prev
        b_scr[pl.ds(r0, SUBLANES), :] = h
        return h[SUBLANES - 1:SUBLANES, :]

    hc_scr[...] = lax.fori_loop(0, ts // SUBLANES, block, hc_scr[...], unroll=4)
    o_ref[...] = (b_scr[...] * jax.nn.gelu(ug)).astype(o_ref.dtype)


def _lru_branch(u, conv_w, conv_b, wa_bd, ba, wx_bd, bx, lam, B, S, ts=256):
    W = LRU_WIDTH
    row = lambda v: v.reshape(1, W)
    full = lambda shape: pl.BlockSpec(shape, lambda b, t: (0,) * len(shape))
    return pl.pallas_call(
        functools.partial(_lru_kernel, ts=ts),
        grid=(B, S // ts),
        in_specs=[
            pl.BlockSpec((None, ts, 2 * W), lambda b, t: (b, t, 0)),
            full((CONV_WIDTH, W)), full((1, W)),
            full((W, W)), full((1, W)), full((W, W)), full((1, W)), full((1, W)),
        ],
        out_specs=pl.BlockSpec((None, ts, W), lambda b, t: (b, t, 0)),
        out_shape=jax.ShapeDtypeStruct((B, S, W), BF16),
        scratch_shapes=[pltpu.VMEM((ts + 8, W), F32), pltpu.VMEM((1, W), F32), pltpu.VMEM((ts, W), F32),
                        pltpu.VMEM((ts, W), F32)],
        compiler_params=_cparams(("parallel", "arbitrary")),
        name="lru_scan",
    )(u.reshape(B, S, 2 * W), conv_w, row(conv_b), wa_bd, row(ba), wx_bd, row(bx), row(lam))


def _rope_slot(x, c, s1, s2):
    return x * c + pltpu.roll(x, LANES - MLA_ROPE // 2, 1) * s1 + pltpu.roll(x, MLA_ROPE // 2, 1) * s2


def _bf16_terms(c):
    hi = c.astype(BF16).astype(F32)
    r = c - hi
    lo = r.astype(BF16).astype(F32)
    return hi, lo, (r - lo).astype(BF16).astype(F32)


def _fox_bias_tables():
    rows = jnp.arange(3 * LANES)
    term, head = rows // LANES, rows % LANES - FLOGIT_LANE0
    valid = (head >= 0) & (head < FOX_HEADS)
    cols = jnp.arange(FOX_HEADS * HEAD_SLOT)
    col_head, col_lane = cols // HEAD_SLOT, cols % HEAD_SLOT - FOX_HEAD_DIM
    hit = valid[:, None] & (head[:, None] == col_head[None, :])
    sel_q = jnp.where(hit & (col_lane[None, :] == term[:, None]), 1.0, 0.0).astype(BF16)
    sel_k = jnp.where(hit & (col_lane[None, :] == term[:, None] + 3), -1.0, 0.0).astype(BF16)
    one_q = jnp.where((col_lane >= 3) & (col_lane < 6), 1.0, 0.0).astype(F32)[None, :]
    one_k = jnp.where((col_lane >= 0) & (col_lane < 3), 1.0, 0.0).astype(F32)[None, :]
    return sel_q, sel_k, one_q, one_k


def _store_values_t(vt_ref, v, ts):
    vt = v.T
    row = lax.broadcasted_iota(jnp.int32, (V_ROWS - MLA_V, ts), 0)
    ones_tile = jnp.where(row == 0, 1.0, 0.0).astype(vt_ref.dtype)
    for h in range(MLA_HEADS):
        vt_ref[h * V_ROWS:h * V_ROWS + MLA_V, :] = vt[h * MLA_V:(h + 1) * MLA_V, :].astype(vt_ref.dtype)
        vt_ref[h * V_ROWS + MLA_V:(h + 1) * V_ROWS, :] = ones_tile


def _prep_kernel(lat_ref, fox_ref, cq_t_ref, ck_t_ref, s1_ref, s2_ref, qn_ref, kvn_ref, wq_ref, wk_ref, wv_ref,
                 bf_ref, selq_ref, selk_ref, oneq_ref, onek_ref, qm_ref, km_ref, vtm_ref, qf_ref, kf_ref, vtf_ref,
                 carry_scr, *, ts, q_scale):
    @pl.when(pl.program_id(1) == 0)
    def _():
        carry_scr[...] = jnp.zeros((1, LANES), F32)

    cq_t, ck_t, s1, s2 = cq_t_ref[...], ck_t_ref[...], s1_ref[...], s2_ref[...]

    cqn = _rms(lat_ref[:, 0:MLA_Q_LORA], qn_ref[...]).astype(BF16)
    q = jnp.dot(cqn, wq_ref[...], preferred_element_type=F32)
    for h in range(MLA_HEADS):
        sl = slice(h * HEAD_SLOT, (h + 1) * HEAD_SLOT)
        qm_ref[:, sl] = (_rope_slot(q[:, sl], cq_t, s1, s2) * q_scale).astype(qm_ref.dtype)

    ckvn = _rms(lat_ref[:, MLA_Q_LORA:MLA_Q_LORA + MLA_KV_LORA], kvn_ref[...]).astype(BF16)
    kn = jnp.dot(ckvn, wk_ref[...], preferred_element_type=F32)
    misc = lat_ref[:, MLA_Q_LORA + MLA_KV_LORA:MLA_Q_LORA + MLA_KV_LORA + LANES]
    kr = _rope_slot(misc, ck_t, s1, s2)
    for h in range(MLA_HEADS):
        sl = slice(h * HEAD_SLOT, (h + 1) * HEAD_SLOT)
        km_ref[:, sl] = (kn[:, sl] + kr).astype(km_ref.dtype)
    _store_values_t(vtm_ref, jnp.dot(ckvn, wv_ref[...], preferred_element_type=F32), ts)

    c = -_softplus(-(misc + bf_ref[...]))
    row = lax.broadcasted_iota(jnp.int32, (ts, LANES), 0)
    d = 1
    while d < ts:
        c = c + jnp.where(row >= d, pltpu.roll(c, d, 0), 0.0)
        d *= 2
    c = c + carry_scr[...]
    carry_scr[...] = c[ts - 1:ts, :]
    c = c * LOG2E

    terms = jnp.concatenate(_bf16_terms(c), axis=1).astype(BF16)
    nslot = FOX_HEADS * HEAD_SLOT
    q_bias = jnp.dot(terms, selq_ref[...], preferred_element_type=F32) + oneq_ref[...]
    k_bias = jnp.dot(terms, selk_ref[...], preferred_element_type=F32) + onek_ref[...]
    qf_ref[...] = (fox_ref[:, 0:nslot].astype(F32) + q_bias).astype(qf_ref.dtype)
    kf_ref[...] = (fox_ref[:, nslot:2 * nslot].astype(F32) + k_bias).astype(kf_ref.dtype)
    _store_values_t(vtf_ref, fox_ref[:, 2 * nslot:2 * nslot + FOX_WIDTH].astype(F32), ts)


def _prep(lat, fox, tabs, qn, kvn, wq, wk, wv, bf_row, bias_tabs, B, S, q_scale, ts=512):
    NL, NF = lat.shape[-1], fox.shape[-1]
    full = lambda shape: pl.BlockSpec(shape, lambda b, t: (0,) * len(shape))
    tab = pl.BlockSpec((ts, LANES), lambda b, t: (t, 0))
    HS = MLA_HEADS * HEAD_SLOT
    slots = pl.BlockSpec((None, ts, HS), lambda b, t: (b, t, 0))
    vts = pl.BlockSpec((None, MLA_HEADS * V_ROWS, ts), lambda b, t: (b, 0, t))
    slot_shape = jax.ShapeDtypeStruct((B, S, HS), BF16)
    vt_shape = jax.ShapeDtypeStruct((B, MLA_HEADS * V_ROWS, S), BF16)
    return pl.pallas_call(
        functools.partial(_prep_kernel, ts=ts, q_scale=q_scale),
        grid=(B, S // ts),
        in_specs=[
            pl.BlockSpec((None, ts, NL), lambda b, t: (b, t, 0)),
            pl.BlockSpec((None, ts, NF), lambda b, t: (b, t, 0)),
            tab, tab, tab, tab,
            full((1, MLA_Q_LORA)), full((1, MLA_KV_LORA)),
            full((MLA_Q_LORA, HS)), full((MLA_KV_LORA, HS)), full((MLA_KV_LORA, MLA_HEADS * MLA_V)),
            full((1, LANES)),
            full((3 * LANES, HS)), full((3 * LANES, HS)), full((1, HS)), full((1, HS)),
        ],
        out_specs=[slots, slots, vts, slots, slots, vts],
        out_shape=[slot_shape, slot_shape, vt_shape, slot_shape, slot_shape, vt_shape],
        scratch_shapes=[pltpu.VMEM((1, LANES), F32)],
        compiler_params=_cparams(("parallel", "arbitrary")),
        name="mla_fox_prep",
    )(lat.reshape(B, S, NL), fox.reshape(B, S, NF), *tabs, qn.reshape(1, -1), kvn.reshape(1, -1), wq, wk, wv,
      bf_row, *bias_tabs)


def _attn_kernel(q_ref, k_ref, vt_ref, o_ref, m_scr, acc_scr, sa_scr, sb_scr, *, tq, tk, unit):
    qi = pl.program_id(2)
    nsub = tq // Q_SUB
    ndiag = tq // tk
    m_scr[...] = jnp.full(m_scr.shape, NEG_INF, F32)
    acc_scr[...] = jnp.zeros(acc_scr.shape, F32)
    all_visible = [0] * nsub

    def diag_kinds(d):
        kinds = []
        for c in range(nsub):
            if (c + 1) * Q_SUB <= d * tk:
                kinds.append(None)
            elif (d + 1) * tk <= c * Q_SUB:
                kinds.append(0)
            else:
                kinds.append((d * tk,))
        return kinds

    def streams(kinds):
        return [(h, c) for c in range(nsub) for h in range(2) if kinds[c] is not None]

    def logits_one(s_scr, k0, h, c):
        kh = k_ref[pl.ds(k0, tk), h * HEAD_SLOT:(h + 1) * HEAD_SLOT]
        qh = q_ref[c * Q_SUB:(c + 1) * Q_SUB, h * HEAD_SLOT:(h + 1) * HEAD_SLOT]
        s_scr[2 * c + h] = lax.dot_general(kh, qh, (((1,), (1,)), ((), ())), preferred_element_type=F32)

    def softmax_one(s_scr, k0, kind, h, c):
        st = s_scr[2 * c + h]
        cs = slice(c * Q_SUB, (c + 1) * Q_SUB)
        if kind != 0:
            kk = lax.broadcasted_iota(jnp.int32, (tk, Q_SUB), 0) + kind[0]
            qq = lax.broadcasted_iota(jnp.int32, (tk, Q_SUB), 1) + c * Q_SUB
            if unit > 1:
                sh = unit.bit_length() - 1
                kk, qq = kk >> sh, qq >> sh
            st = jnp.where(qq >= kk, st, NEG_INF)
        m_prev = m_scr[h, :, cs]
        m_new = jnp.maximum(m_prev, jnp.max(st, axis=0, keepdims=True))
        alpha = jnp.exp2(m_prev - m_new)
        pt = jnp.exp2(st - m_new).astype(BF16)
        m_scr[h, :, cs] = m_new
        vth = vt_ref[h * V_ROWS:(h + 1) * V_ROWS, pl.ds(k0, tk)]
        acc_scr[h, :, cs] = alpha * acc_scr[h, :, cs] + jnp.dot(vth, pt, preferred_element_type=F32)

    def stage(nxt, k_nxt, kinds_nxt, cur, k_cur, kinds_cur):
        sn = streams(kinds_nxt) if nxt is not None else []
        sc = streams(kinds_cur) if cur is not None else []
        for idx in range(max(len(sn), len(sc))):
            if idx < len(sn):
                logits_one(nxt, k_nxt, *sn[idx])
            if idx < len(sc):
                h, c = sc[idx]
                softmax_one(cur, k_cur, kinds_cur[c], h, c)

    n = qi * ndiag
    kd = [pl.multiple_of(qi * tq + d * tk, tk) for d in range(ndiag)]
    bufs = [sa_scr, sb_scr]
    stage(bufs[0], kd[0], diag_kinds(0), None, None, None)
    for d in range(ndiag):
        if d + 1 < ndiag:
            stage(bufs[(d + 1) % 2], kd[d + 1], diag_kinds(d + 1), bufs[d % 2], kd[d], diag_kinds(d))
        else:
            stage(bufs[(d + 1) % 2], 0, all_visible, bufs[d % 2], kd[d], diag_kinds(d))
    first = bufs[ndiag % 2]
    second = bufs[(ndiag + 1) % 2]

    def body(i, carry):
        k_a = pl.multiple_of(2 * i * tk, tk)
        k_b = pl.multiple_of((2 * i + 1) * tk, tk)
        k_c = pl.multiple_of(jnp.minimum(2 * i + 2, n - 1) * tk, tk)
        stage(second, k_b, all_visible, first, k_a, all_visible)
        stage(first, k_c, all_visible, second, k_b, all_visible)
        return carry

    lax.fori_loop(0, n // 2, body, 0)

    outs = []
    for h in range(2):
        a = acc_scr[h]
        outs.append(a[0:MLA_V, :] / a[MLA_V:MLA_V + 1, :])
    o_ref[...] = jnp.concatenate(outs, axis=0).T.astype(o_ref.dtype)


def _attention(q, k, vt, B, S, *, unit, name, tq=1024, tk=512):
    npair = MLA_HEADS // 2
    nstream = 2 * (tq // Q_SUB)
    assert (tq // tk) % 2 == 0
    return pl.pallas_call(
        functools.partial(_attn_kernel, tq=tq, tk=tk, unit=unit),
        grid=(B, npair, S // tq),
        in_specs=[
            pl.BlockSpec((None, tq, 2 * HEAD_SLOT), lambda b, p, i: (b, i, p)),
            pl.BlockSpec((None, S, 2 * HEAD_SLOT), lambda b, p, i: (b, 0, p)),
            pl.BlockSpec((None, 2 * V_ROWS, S), lambda b, p, i: (b, p, 0)),
        ],
        out_specs=pl.BlockSpec((None, tq, 2 * MLA_V), lambda b, p, i: (b, i, p)),
        out_shape=jax.ShapeDtypeStruct((B, S, npair * 2 * MLA_V), BF16),
        scratch_shapes=[pltpu.VMEM((2, 1, tq), F32), pltpu.VMEM((2, V_ROWS, tq), F32),
                        pltpu.VMEM((nstream, tk, Q_SUB), F32), pltpu.VMEM((nstream, tk, Q_SUB), F32)],
        compiler_params=_cparams(("parallel", "parallel", "arbitrary")),
        name=name,
    )(q, k, vt)


def _merge_kernel(ya_ref, ob_ref, oc_ref, g_ref, x_ref, wa_ref, wb_ref, wc_ref, wo_ref, o_ref):
    D = x_ref.shape[-1]
    ya = jnp.dot(ya_ref[...], wa_ref[...], preferred_element_type=F32)
    yb = jnp.dot(ob_ref[...], wb_ref[...], preferred_element_type=F32)
    yc = jnp.dot(oc_ref[...], wc_ref[...], preferred_element_type=F32)
    merged = g_ref[:, 0:D] * ya + g_ref[:, D:2 * D] * yb + g_ref[:, 2 * D:3 * D] * yc
    o_ref[...] = x_ref[...] + jnp.dot(merged.astype(BF16), wo_ref[...], preferred_element_type=F32)


def _merge(ya, ob, oc, g, x, wa, wb, wc, wo, tm=512):
    T, D = x.shape
    W = ya.shape[-1]
    rows = lambda n: pl.BlockSpec((tm, n), lambda i: (i, 0))
    return pl.pallas_call(
        _merge_kernel,
        grid=(T // tm,),
        in_specs=[rows(W), rows(W), rows(W), rows(3 * D), rows(D),
                  _resident((W, D)), _resident((W, D)), _resident((W, D)), _resident((D, D))],
        out_specs=rows(D),
        out_shape=jax.ShapeDtypeStruct((T, D), F32),
        compiler_params=_cparams(("parallel",)),
        name="merge",
    )(ya, ob, oc, g, x, wa, wb, wc, wo)


FFN_TF = 1408


def _ffn_ple_kernel(x_ref, p_ref, fg_ref, wgu_ref, wd_ref, pn_ref, wpg_ref, wple_ref, fn_ref, o_ref, *, final):
    d_ff = wd_ref.shape[0]
    x = x_ref[...]
    h = _rms(x, fg_ref[...]).astype(BF16)
    for c in range(d_ff // FFN_TF):
        gate = jnp.dot(h, wgu_ref[:, c * FFN_TF:(c + 1) * FFN_TF], preferred_element_type=F32)
        up = jnp.dot(h, wgu_ref[:, d_ff + c * FFN_TF:d_ff + (c + 1) * FFN_TF], preferred_element_type=F32)
        act = (gate * _sigmoid(gate) * up).astype(BF16)
        x = x + jnp.dot(act, wd_ref[c * FFN_TF:(c + 1) * FFN_TF, :], preferred_element_type=F32)
    hn = _rms(x, pn_ref[...]).astype(BF16)
    pg = _sigmoid(jnp.dot(hn, wpg_ref[...], preferred_element_type=F32))
    pe = jnp.dot(p_ref[...].astype(BF16), wple_ref[...], preferred_element_type=F32)
    x = x + pg * pe
    if final:
        x = _rms(x, fn_ref[...])
    o_ref[...] = x


def _ffn_ple(x, p, fg, w_gate_up, w_down, pn, wpg, wple, fn, final, tm=512):
    T, D = x.shape
    P = p.shape[-1]
    d_ff = w_down.shape[0]
    assert d_ff % FFN_TF == 0
    return pl.pallas_call(
        functools.partial(_ffn_ple_kernel, final=final),
        grid=(T // tm,),
        in_specs=[pl.BlockSpec((tm, D), lambda i: (i, 0)), pl.BlockSpec((tm, P), lambda i: (i, 0)),
                  _resident((1, D)), _resident((D, 2 * d_ff)), _resident((d_ff, D)),
                  _resident((1, D)), _resident((D, D)), _resident((P, D)), _resident((1, D))],
        out_specs=pl.BlockSpec((tm, D), lambda i: (i, 0)),
        out_shape=jax.ShapeDtypeStruct((T, D), F32),
        compiler_params=_cparams(("parallel",)),
        name="ffn_ple_final" if final else "ffn_ple",
    )(x, p, fg.reshape(1, D), w_gate_up, w_down, pn.reshape(1, D), wpg, wple, fn.reshape(1, D))


def _block_diag(w):
    H, I, J = w.shape
    eye = jnp.eye(H, dtype=w.dtype)
    return (eye[:, None, :, None] * w[:, :, None, :]).reshape(H * I, H * J)


def _head_slots(w, width):
    K = w.shape[0]
    w = w.reshape(K, MLA_HEADS, width)
    return jnp.pad(w, ((0, 0), (0, 0), (0, HEAD_SLOT - width))).reshape(K, MLA_HEADS * HEAD_SLOT)


def _rope_tables(S):
    pos = jnp.arange(S, dtype=F32)
    inv_freq = ROPE_BASE ** (-jnp.arange(0, MLA_ROPE, 2, dtype=F32) / MLA_ROPE)
    ang = pos[:, None] * inv_freq[None, :]
    cos, sin = jnp.cos(ang), jnp.sin(ang)
    half = MLA_ROPE // 2
    z = lambda n: jnp.zeros((S, n), F32)
    tail = LANES - ROPE_LANE0 - MLA_ROPE
    cq_t = jnp.concatenate([jnp.ones((S, ROPE_LANE0), F32), cos, cos, z(tail)], axis=1)
    ck_t = jnp.concatenate([z(ROPE_LANE0), cos, cos, z(tail)], axis=1)
    s1 = jnp.concatenate([z(ROPE_LANE0), -sin, z(half), z(tail)], axis=1)
    s2 = jnp.concatenate([z(ROPE_LANE0), z(half), sin, z(tail)], axis=1)
    return cq_t, ck_t, s1, s2


def kernel(x, p, mix_norm, w_in, gate_b, conv_w, conv_b, lru_wa, lru_ba, lru_wx, lru_bx, lru_lambda,
           mla_q_norm, mla_wuq, mla_kv_norm, mla_wukv, fox_bf, w_br_a, w_br_b, w_br_c, w_o,
           ffn_norm, w_gate_up, w_down, ple_norm, w_ple_gate, w_ple, final_norm):
    B, S, D = x.shape
    depth = w_in.shape[0]
    T = B * S
    tabs = _rope_tables(S)
    bias_tabs = _fox_bias_tables()
    mla_scale = (MLA_NOPE + MLA_ROPE) ** -0.5 * LOG2E
    fox_scale = FOX_HEAD_DIM ** -0.5 * LOG2E

    o_u = 0
    o_cq = 2 * LRU_WIDTH
    o_ckv = o_cq + MLA_Q_LORA
    o_kr = o_ckv + MLA_KV_LORA
    o_fq = o_kr + MLA_ROPE
    o_fl = o_fq + 3 * FOX_WIDTH
    o_g = o_fl + FOX_HEADS

    xf = x.reshape(T, D)
    for i in range(depth):
        wi = w_in[i]
        zcols = lambda n: jnp.zeros((D, n), F32)
        lat_w = MLA_Q_LORA + MLA_KV_LORA + LANES
        lat_pad = -lat_w % IN_TN
        nslot = FOX_HEADS * HEAD_SLOT
        w_all = jnp.concatenate([
            wi[:, o_u:o_cq],
            wi[:, o_cq:o_kr], zcols(ROPE_LANE0), wi[:, o_kr:o_fq], wi[:, o_fl:o_g],
            zcols(LANES - ROPE_LANE0 - MLA_ROPE - FOX_HEADS + lat_pad),
            _head_slots(wi[:, o_fq:o_fq + FOX_WIDTH], FOX_HEAD_DIM),
            _head_slots(wi[:, o_fq + FOX_WIDTH:o_fq + 2 * FOX_WIDTH], FOX_HEAD_DIM),
            wi[:, o_fq + 2 * FOX_WIDTH:o_fl],
            wi[:, o_g:]], axis=1).astype(BF16)
        widths = (2 * LRU_WIDTH, lat_w + lat_pad, 2 * nslot + FOX_WIDTH, 3 * D)
        n_before_fox, n_before_gate = widths[0] + widths[1], widths[0] + widths[1] + widths[2]
        cs = jnp.ones((1, sum(widths)), F32).at[0, n_before_fox:n_before_fox + nslot].set(fox_scale)
        cb = jnp.zeros((1, sum(widths)), F32).at[0, n_before_gate:].set(gate_b[i])
        u, lat, fox, g = _in_proj(xf, mix_norm[i], w_all, cs, cb, widths)

        ya = _lru_branch(u, conv_w[i], conv_b[i], _block_diag(lru_wa[i]).astype(BF16), lru_ba[i],
                         _block_diag(lru_wx[i]).astype(BF16), lru_bx[i], lru_lambda[i], B, S)

        wq = _head_slots(mla_wuq[i], MLA_NOPE + MLA_ROPE).astype(BF16)
        wukv = mla_wukv[i].reshape(MLA_KV_LORA, MLA_HEADS, MLA_NOPE + MLA_V)
        wk = _head_slots(wukv[:, :, :MLA_NOPE].reshape(MLA_KV_LORA, MLA_HEADS * MLA_NOPE), MLA_NOPE).astype(BF16)
        wv = wukv[:, :, MLA_NOPE:].reshape(MLA_KV_LORA, MLA_HEADS * MLA_V).astype(BF16)
        bf_row = jnp.zeros((1, LANES), F32).at[0, FLOGIT_LANE0:FLOGIT_LANE0 + FOX_HEADS].set(fox_bf[i])
        q_m, k_m, vt_m, q_f, k_f, vt_f = _prep(lat, fox, tabs, mla_q_norm[i], mla_kv_norm[i], wq, wk, wv, bf_row,
                                               bias_tabs, B, S, mla_scale)

        ob = _attention(q_m, k_m, vt_m, B, S, unit=MLA_CHUNK, name="mla_attention")
        oc = _attention(q_f, k_f, vt_f, B, S, unit=1, name="fox_attention")

        xf = _merge(ya.reshape(T, -1), ob.reshape(T, -1), oc.reshape(T, -1), g, xf,
                    w_br_a[i].astype(BF16), w_br_b[i].astype(BF16), w_br_c[i].astype(BF16), w_o[i].astype(BF16))
        xf = _ffn_ple(xf, p[i].reshape(T, -1), ffn_norm[i], w_gate_up[i].astype(BF16), w_down[i].astype(BF16),
                      ple_norm[i], w_ple_gate[i].astype(BF16), w_ple[i].astype(BF16), final_norm,
                      final=(i == depth - 1))
    return xf.reshape(B, S, D)
```

```python
import functools
import math

import jax
import jax.numpy as jnp
from jax import lax
from jax.experimental import pallas as pl
from jax.experimental.pallas import tpu as pltpu

F32 = jnp.float32
BF16 = jnp.bfloat16

EPS = 1e-6
NEG_INF = -1e30
LOG2E = math.log2(math.e)

LRU_WIDTH = 512
LRU_HEADS = 8
CONV_WIDTH = 4
LRU_C = 8.0

MLA_HEADS = 8
MLA_Q_LORA = 384
MLA_KV_LORA = 256
MLA_NOPE = 64
MLA_ROPE = 32
MLA_V = 64
ROPE_BASE = 10000.0
MLA_CHUNK = 64

FOX_HEADS = 8
FOX_HEAD_DIM = 64
FOX_WIDTH = FOX_HEADS * FOX_HEAD_DIM

LANES = 128
SUBLANES = 8
HEAD_SLOT = LANES
ROPE_LANE0 = MLA_NOPE
FLOGIT_LANE0 = MLA_NOPE + MLA_ROPE
V_ROWS = 80
Q_SUB = 256
SKEW = 4

VMEM_LIMIT = 56 * 1024 * 1024


def _cparams(sem):
    return pltpu.CompilerParams(dimension_semantics=sem, vmem_limit_bytes=VMEM_LIMIT)


def _rms(xf, g):
    ms = jnp.mean(xf * xf, axis=-1, keepdims=True)
    return xf * lax.rsqrt(ms + EPS) * g


def _sigmoid(z):
    return 0.5 * jnp.tanh(0.5 * z) + 0.5


def _softplus(z):
    return jnp.maximum(z, 0.0) + jnp.log1p(jnp.exp(-jnp.abs(z)))


IN_TN = 512


def _in_proj_kernel(x_ref, g_ref, w_ref, cs_ref, cb_ref, u_ref, lat_ref, fox_ref, gate_ref, *, widths):
    h = _rms(x_ref[...], g_ref[...]).astype(BF16)
    outs = (u_ref, lat_ref, fox_ref, gate_ref)
    col = 0
    for slab, (o_ref, width) in enumerate(zip(outs, widths)):
        for j in range(width // IN_TN):
            sl = slice(col + j * IN_TN, col + (j + 1) * IN_TN)
            acc = jnp.dot(h, w_ref[:, sl], preferred_element_type=F32)
            if slab == 2:
                acc = acc * cs_ref[:, sl]
            if slab == 3:
                acc = _sigmoid(acc + cb_ref[:, sl])
            o_ref[:, j * IN_TN:(j + 1) * IN_TN] = acc.astype(o_ref.dtype)
        col += width


def _resident(shape):
    return pl.BlockSpec(shape, lambda i: (0, 0), pipeline_mode=pl.Buffered(1))


def _in_proj(x, g, w_all, cs, cb, widths, tm=512):
    T, D = x.shape
    N = sum(widths)
    return pl.pallas_call(
        functools.partial(_in_proj_kernel, widths=widths),
        grid=(T // tm,),
        in_specs=[
            pl.BlockSpec((tm, D), lambda i: (i, 0)),
            _resident((1, D)), _resident((D, N)), _resident((1, N)), _resident((1, N)),
        ],
        out_specs=[pl.BlockSpec((tm, w), lambda i: (i, 0)) for w in widths],
        out_shape=[jax.ShapeDtypeStruct((T, widths[0]), BF16), jax.ShapeDtypeStruct((T, widths[1]), F32),
                   jax.ShapeDtypeStruct((T, widths[2]), BF16), jax.ShapeDtypeStruct((T, widths[3]), BF16)],
        compiler_params=_cparams(("parallel",)),
        name="in_proj",
    )(x, g.reshape(1, D), w_all, cs, cb)


def _lru_kernel(u_ref, cw_ref, cb_ref, wa_ref, ba_ref, wx_ref, bx_ref, lam_ref, o_ref, ext_scr, hc_scr, a_scr,
                b_scr, *, ts):
    W = LRU_WIDTH
    pad = 8

    @pl.when(pl.program_id(1) == 0)
    def _():
        ext_scr[0:pad, :] = jnp.zeros((pad, W), F32)
        hc_scr[...] = jnp.zeros((1, W), F32)

    u = u_ref[:, 0:W].astype(F32)
    ug = u_ref[:, W:2 * W].astype(F32)
    ext_scr[pad:pad + ts, :] = u
    xc = cb_ref[...] + ext_scr[pl.ds(pad - 3, ts), :] * cw_ref[0:1, :]
    xc = xc + ext_scr[pl.ds(pad - 2, ts), :] * cw_ref[1:2, :]
    xc = xc + ext_scr[pl.ds(pad - 1, ts), :] * cw_ref[2:3, :]
    xc = xc + u * cw_ref[3:4, :]
    ext_scr[0:pad, :] = u[ts - pad:ts, :]

    xb = xc.astype(BF16)
    r = _sigmoid(jnp.dot(xb, wa_ref[...], preferred_element_type=F32) + ba_ref[...])
    ig = _sigmoid(jnp.dot(xb, wx_ref[...], preferred_element_type=F32) + bx_ref[...])
    log_a = (-LRU_C) * r * _softplus(-lam_ref[...])
    a = jnp.exp(log_a)
    b = jnp.sqrt(jnp.tanh(-log_a) * (a * a + 1.0)) * (ig * xc)

    a_scr[...] = a
    b_scr[...] = b
    row = lax.broadcasted_iota(jnp.int32, (SUBLANES, W), 0)

    def block(i, h_prev):
        r0 = pl.multiple_of(i * SUBLANES, SUBLANES)
        aa = a_scr[pl.ds(r0, SUBLANES), :]
        bb = b_scr[pl.ds(r0, SUBLANES), :]
        d = 1
        while d < SUBLANES:
            keep = row >= d
            a_sh = jnp.where(keep, pltpu.roll(aa, d, 0), 1.0)
            b_sh = jnp.where(keep, pltpu.roll(bb, d, 0), 0.0)
            bb = aa * b_sh + bb
            aa = aa * a_sh
            d *= 2
        h = bb + aa * h_prev
        b_scr[pl.ds(r0, SUBLANES), :] = h
        return h[SUBLANES - 1:SUBLANES, :]

    hc_scr[...] = lax.fori_loop(0, ts // SUBLANES, block, hc_scr[...], unroll=4)
    o_ref[...] = (b_scr[...] * jax.nn.gelu(ug)).astype(o_ref.dtype)


def _lru_branch(u, conv_w, conv_b, wa_bd, ba, wx_bd, bx, lam, B, S, ts=256):
    W = LRU_WIDTH
    row = lambda v: v.reshape(1, W)
    full = lambda shape: pl.BlockSpec(shape, lambda b, t: (0,) * len(shape))
    return pl.pallas_call(
        functools.partial(_lru_kernel, ts=ts),
        grid=(B, S // ts),
        in_specs=[
            pl.BlockSpec((None, ts, 2 * W), lambda b, t: (b, t, 0)),
            full((CONV_WIDTH, W)), full((1, W)),
            full((W, W)), full((1, W)), full((W, W)), full((1, W)), full((1, W)),
        ],
        out_specs=pl.BlockSpec((None, ts, W), lambda b, t: (b, t, 0)),
        out_shape=jax.ShapeDtypeStruct((B, S, W), BF16),
        scratch_shapes=[pltpu.VMEM((ts + 8, W), F32), pltpu.VMEM((1, W), F32), pltpu.VMEM((ts, W), F32),
                        pltpu.VMEM((ts, W), F32)],
        compiler_params=_cparams(("parallel", "arbitrary")),
        name="lru_scan",
    )(u.reshape(B, S, 2 * W), conv_w, row(conv_b), wa_bd, row(ba), wx_bd, row(bx), row(lam))


def _rope_slot(x, c, s1, s2):
    return x * c + pltpu.roll(x, LANES - MLA_ROPE // 2, 1) * s1 + pltpu.roll(x, MLA_ROPE // 2, 1) * s2


def _bf16_terms(c):
    hi = c.astype(BF16).astype(F32)
    r = c - hi
    lo = r.astype(BF16).astype(F32)
    return hi, lo, (r - lo).astype(BF16).astype(F32)


def _fox_bias_tables():
    rows = jnp.arange(3 * LANES)
    term, head = rows // LANES, rows % LANES - FLOGIT_LANE0
    valid = (head >= 0) & (head < FOX_HEADS)
    cols = jnp.arange(FOX_HEADS * HEAD_SLOT)
    col_head, col_lane = cols // HEAD_SLOT, cols % HEAD_SLOT - FOX_HEAD_DIM
    hit = valid[:, None] & (head[:, None] == col_head[None, :])
    sel_q = jnp.where(hit & (col_lane[None, :] == term[:, None]), 1.0, 0.0).astype(BF16)
    sel_k = jnp.where(hit & (col_lane[None, :] == term[:, None] + 3), -1.0, 0.0).astype(BF16)
    one_q = jnp.where((col_lane >= 3) & (col_lane < 6), 1.0, 0.0).astype(F32)[None, :]
    one_k = jnp.where((col_lane >= 0) & (col_lane < 3), 1.0, 0.0).astype(F32)[None, :]
    return sel_q, sel_k, one_q, one_k


def _store_values_t(vt_ref, v, ts):
    vt = v.T
    row = lax.broadcasted_iota(jnp.int32, (V_ROWS - MLA_V, ts), 0)
    ones_tile = jnp.where(row == 0, 1.0, 0.0).astype(vt_ref.dtype)
    for h in range(MLA_HEADS):
        vt_ref[h * V_ROWS:h * V_ROWS + MLA_V, :] = vt[h * MLA_V:(h + 1) * MLA_V, :].astype(vt_ref.dtype)
        vt_ref[h * V_ROWS + MLA_V:(h + 1) * V_ROWS, :] = ones_tile


def _prep_kernel(lat_ref, fox_ref, cq_t_ref, ck_t_ref, s1_ref, s2_ref, qn_ref, kvn_ref, wq_ref, wk_ref, wv_ref,
                 bf_ref, selq_ref, selk_ref, oneq_ref, onek_ref, qm_ref, km_ref, vtm_ref, qf_ref, kf_ref, vtf_ref,
                 carry_scr, *, ts, q_scale):
    @pl.when(pl.program_id(1) == 0)
    def _():
        carry_scr[...] = jnp.zeros((1, LANES), F32)

    cq_t, ck_t, s1, s2 = cq_t_ref[...], ck_t_ref[...], s1_ref[...], s2_ref[...]

    cqn = _rms(lat_ref[:, 0:MLA_Q_LORA], qn_ref[...]).astype(BF16)
    q = jnp.dot(cqn, wq_ref[...], preferred_element_type=F32)
    for h in range(MLA_HEADS):
        sl = slice(h * HEAD_SLOT, (h + 1) * HEAD_SLOT)
        qm_ref[:, sl] = (_rope_slot(q[:, sl], cq_t, s1, s2) * q_scale).astype(qm_ref.dtype)

    ckvn = _rms(lat_ref[:, MLA_Q_LORA:MLA_Q_LORA + MLA_KV_LORA], kvn_ref[...]).astype(BF16)
    kn = jnp.dot(ckvn, wk_ref[...], preferred_element_type=F32)
    misc = lat_ref[:, MLA_Q_LORA + MLA_KV_LORA:MLA_Q_LORA + MLA_KV_LORA + LANES]
    kr = _rope_slot(misc, ck_t, s1, s2)
    for h in range(MLA_HEADS):
        sl = slice(h * HEAD_SLOT, (h + 1) * HEAD_SLOT)
        km_ref[:, sl] = (kn[:, sl] + kr).astype(km_ref.dtype)
    _store_values_t(vtm_ref, jnp.dot(ckvn, wv_ref[...], preferred_element_type=F32), ts)

    c = -_softplus(-(misc + bf_ref[...]))
    row = lax.broadcasted_iota(jnp.int32, (ts, LANES), 0)
    d = 1
    while d < ts:
        c = c + jnp.where(row >= d, pltpu.roll(c, d, 0), 0.0)
        d *= 2
    c = c + carry_scr[...]
    carry_scr[...] = c[ts - 1:ts, :]
    c = c * LOG2E

    terms = jnp.concatenate(_bf16_terms(c), axis=1).astype(BF16)
    nslot = FOX_HEADS * HEAD_SLOT
    q_bias = jnp.dot(terms, selq_ref[...], preferred_element_type=F32) + oneq_ref[...]
    k_bias = jnp.dot(terms, selk_ref[...], preferred_element_type=F32) + onek_ref[...]
    qf_ref[...] = (fox_ref[:, 0:nslot].astype(F32) + q_bias).astype(qf_ref.dtype)
    kf_ref[...] = (fox_ref[:, nslot:2 * nslot].astype(F32) + k_bias).astype(kf_ref.dtype)
    _store_values_t(vtf_ref, fox_ref[:, 2 * nslot:2 * nslot + FOX_WIDTH].astype(F32), ts)


def _prep(lat, fox, tabs, qn, kvn, wq, wk, wv, bf_row, bias_tabs, B, S, q_scale, ts=512):
    NL, NF = lat.shape[-1], fox.shape[-1]
    full = lambda shape: pl.BlockSpec(shape, lambda b, t: (0,) * len(shape))
    tab = pl.BlockSpec((ts, LANES), lambda b, t: (t, 0))
    HS = MLA_HEADS * HEAD_SLOT
    slots = pl.BlockSpec((None, ts, HS), lambda b, t: (b, t, 0))
    vts = pl.BlockSpec((None, MLA_HEADS * V_ROWS, ts), lambda b, t: (b, 0, t))
    slot_shape = jax.ShapeDtypeStruct((B, S, HS), BF16)
    vt_shape = jax.ShapeDtypeStruct((B, MLA_HEADS * V_ROWS, S), BF16)
    return pl.pallas_call(
        functools.partial(_prep_kernel, ts=ts, q_scale=q_scale),
        grid=(B, S // ts),
        in_specs=[
            pl.BlockSpec((None, ts, NL), lambda b, t: (b, t, 0)),
            pl.BlockSpec((None, ts, NF), lambda b, t: (b, t, 0)),
            tab, tab, tab, tab,
            full((1, MLA_Q_LORA)), full((1, MLA_KV_LORA)),
            full((MLA_Q_LORA, HS)), full((MLA_KV_LORA, HS)), full((MLA_KV_LORA, MLA_HEADS * MLA_V)),
            full((1, LANES)),
            full((3 * LANES, HS)), full((3 * LANES, HS)), full((1, HS)), full((1, HS)),
        ],
        out_specs=[slots, slots, vts, slots, slots, vts],
        out_shape=[slot_shape, slot_shape, vt_shape, slot_shape, slot_shape, vt_shape],
        scratch_shapes=[pltpu.VMEM((1, LANES), F32)],
        compiler_params=_cparams(("parallel", "arbitrary")),
        name="mla_fox_prep",
    )(lat.reshape(B, S, NL), fox.reshape(B, S, NF), *tabs, qn.reshape(1, -1), kvn.reshape(1, -1), wq, wk, wv,
      bf_row, *bias_tabs)


def _attn_kernel(q_ref, k_ref, vt_ref, o_ref, m_scr, acc_scr, s_scr, *, tq, tk, unit):
    qi = pl.program_id(2)
    nsub = tq // Q_SUB
    ndiag = tq // tk
    m_scr[...] = jnp.full(m_scr.shape, NEG_INF, F32)
    acc_scr[...] = jnp.zeros(acc_scr.shape, F32)

    def diag_kinds(d):
        kinds = []
        for c in range(nsub):
            if (c + 1) * Q_SUB <= d * tk:
                kinds.append(None)
            elif (d + 1) * tk <= c * Q_SUB:
                kinds.append((tk, None))
            else:
                kinds.append((min(tk, (c + 1) * Q_SUB - d * tk), d * tk))
        return kinds

    def units(k0, kinds, slot0):
        live = [(h, c) for c in range(nsub) for h in range(2) if kinds[c] is not None]
        return [(k0, kinds[c][0], kinds[c][1], h, c, slot0 + idx) for idx, (h, c) in enumerate(live)]

    def logits_one(k0, klen, h, c, slot):
        kh = k_ref[pl.ds(k0, klen), h * HEAD_SLOT:(h + 1) * HEAD_SLOT]
        qh = q_ref[c * Q_SUB:(c + 1) * Q_SUB, h * HEAD_SLOT:(h + 1) * HEAD_SLOT]
        s_scr[slot, 0:klen] = lax.dot_general(kh, qh, (((1,), (1,)), ((), ())), preferred_element_type=F32)

    def softmax_one(k0, klen, mask_off, h, c, slot):
        st = s_scr[slot, 0:klen]
        cs = slice(c * Q_SUB, (c + 1) * Q_SUB)
        if mask_off is not None:
            kk = lax.broadcasted_iota(jnp.int32, (klen, Q_SUB), 0) + mask_off
            qq = lax.broadcasted_iota(jnp.int32, (klen, Q_SUB), 1) + c * Q_SUB
            if unit > 1:
                sh = unit.bit_length() - 1
                kk, qq = kk >> sh, qq >> sh
            st = jnp.where(qq >= kk, st, NEG_INF)
        m_prev = m_scr[h, :, cs]
        m_new = jnp.maximum(m_prev, jnp.max(st, axis=0, keepdims=True))
        alpha = jnp.exp2(m_prev - m_new)
        pt = jnp.exp2(st - m_new).astype(BF16)
        m_scr[h, :, cs] = m_new
        vth = vt_ref[h * V_ROWS:(h + 1) * V_ROWS, pl.ds(k0, klen)]
        acc_scr[h, :, cs] = alpha * acc_scr[h, :, cs] + jnp.dot(vth, pt, preferred_element_type=F32)

    def run(todo, ahead):
        for j in range(len(todo)):
            if j < len(ahead):
                k0, klen, _, h, c, slot = ahead[j]
                logits_one(k0, klen, h, c, slot)
            softmax_one(*todo[j])

    n = qi * ndiag
    per = 2 * nsub
    all_visible = [(tk, None)] * nsub
    for k0, klen, _, h, c, slot in units(0, all_visible, 0)[:SKEW]:
        logits_one(k0, klen, h, c, slot)

    def body(i, carry):
        chunks = [units(pl.multiple_of((2 * i + t) * tk, tk), all_visible, (t % 2) * per) for t in range(3)]
        flat = chunks[0] + chunks[1] + chunks[2]
        run(flat[:2 * per], flat[SKEW:SKEW + 2 * per])
        return carry

    lax.fori_loop(0, n // 2, body, 0)
    flat = []
    for d in range(ndiag):
        flat += units(pl.multiple_of(qi * tq + d * tk, tk), diag_kinds(d), (d % 2) * per)
    run(flat, flat[SKEW:])

    outs = []
    for h in range(2):
        a = acc_scr[h]
        outs.append(a[0:MLA_V, :] / a[MLA_V:MLA_V + 1, :])
    o_ref[...] = jnp.concatenate(outs, axis=0).T.astype(o_ref.dtype)


def _attention(q, k, vt, B, S, *, unit, name, tq=2048, tk=512):
    npair = MLA_HEADS // 2
    nstream = 2 * (tq // Q_SUB)
    assert (tq // tk) % 2 == 0
    return pl.pallas_call(
        functools.partial(_attn_kernel, tq=tq, tk=tk, unit=unit),
        grid=(B, npair, S // tq),
        in_specs=[
            pl.BlockSpec((None, tq, 2 * HEAD_SLOT), lambda b, p, i: (b, i, p)),
            pl.BlockSpec((None, S, 2 * HEAD_SLOT), lambda b, p, i: (b, 0, p)),
            pl.BlockSpec((None, 2 * V_ROWS, S), lambda b, p, i: (b, p, 0)),
        ],
        out_specs=pl.BlockSpec((None, tq, 2 * MLA_V), lambda b, p, i: (b, i, p)),
        out_shape=jax.ShapeDtypeStruct((B, S, npair * 2 * MLA_V), BF16),
        scratch_shapes=[pltpu.VMEM((2, 1, tq), F32), pltpu.VMEM((2, V_ROWS, tq), F32),
                        pltpu.VMEM((2 * nstream, tk, Q_SUB), F32)],
        compiler_params=_cparams(("parallel", "parallel", "arbitrary")),
        name=name,
    )(q, k, vt)


def _merge_kernel(ya_ref, ob_ref, oc_ref, g_ref, x_ref, wa_ref, wb_ref, wc_ref, wo_ref, o_ref):
    D = x_ref.shape[-1]
    ya = jnp.dot(ya_ref[...], wa_ref[...], preferred_element_type=F32)
    yb = jnp.dot(ob_ref[...], wb_ref[...], preferred_element_type=F32)
    yc = jnp.dot(oc_ref[...], wc_ref[...], preferred_element_type=F32)
    merged = g_ref[:, 0:D] * ya + g_ref[:, D:2 * D] * yb + g_ref[:, 2 * D:3 * D] * yc
    o_ref[...] = x_ref[...] + jnp.dot(merged.astype(BF16), wo_ref[...], preferred_element_type=F32)


def _merge(ya, ob, oc, g, x, wa, wb, wc, wo, tm=512):
    T, D = x.shape
    W = ya.shape[-1]
    rows = lambda n: pl.BlockSpec((tm, n), lambda i: (i, 0))
    return pl.pallas_call(
        _merge_kernel,
        grid=(T // tm,),
        in_specs=[rows(W), rows(W), rows(W), rows(3 * D), rows(D),
                  _resident((W, D)), _resident((W, D)), _resident((W, D)), _resident((D, D))],
        out_specs=rows(D),
        out_shape=jax.ShapeDtypeStruct((T, D), F32),
        compiler_params=_cparams(("parallel",)),
        name="merge",
    )(ya, ob, oc, g, x, wa, wb, wc, wo)


FFN_TF = 1408


def _ffn_ple_kernel(x_ref, p_ref, fg_ref, wgu_ref, wd_ref, pn_ref, wpg_ref, wple_ref, fn_ref, o_ref, *, final):
    d_ff = wd_ref.shape[0]
    x = x_ref[...]
    h = _rms(x, fg_ref[...]).astype(BF16)
    for c in range(d_ff // FFN_TF):
        gate = jnp.dot(h, wgu_ref[:, c * FFN_TF:(c + 1) * FFN_TF], preferred_element_type=F32)
        up = jnp.dot(h, wgu_ref[:, d_ff + c * FFN_TF:d_ff + (c + 1) * FFN_TF], preferred_element_type=F32)
        act = (gate * _sigmoid(gate) * up).astype(BF16)
        x = x + jnp.dot(act, wd_ref[c * FFN_TF:(c + 1) * FFN_TF, :], preferred_element_type=F32)
    hn = _rms(x, pn_ref[...]).astype(BF16)
    pg = _sigmoid(jnp.dot(hn, wpg_ref[...], preferred_element_type=F32))
    pe = jnp.dot(p_ref[...].astype(BF16), wple_ref[...], preferred_element_type=F32)
    x = x + pg * pe
    if final:
        x = _rms(x, fn_ref[...])
    o_ref[...] = x


def _ffn_ple(x, p, fg, w_gate_up, w_down, pn, wpg, wple, fn, final, tm=512):
    T, D = x.shape
    P = p.shape[-1]
    d_ff = w_down.shape[0]
    assert d_ff % FFN_TF == 0
    return pl.pallas_call(
        functools.partial(_ffn_ple_kernel, final=final),
        grid=(T // tm,),
        in_specs=[pl.BlockSpec((tm, D), lambda i: (i, 0)), pl.BlockSpec((tm, P), lambda i: (i, 0)),
                  _resident((1, D)), _resident((D, 2 * d_ff)), _resident((d_ff, D)),
                  _resident((1, D)), _resident((D, D)), _resident((P, D)), _resident((1, D))],
        out_specs=pl.BlockSpec((tm, D), lambda i: (i, 0)),
        out_shape=jax.ShapeDtypeStruct((T, D), F32),
        compiler_params=_cparams(("parallel",)),
        name="ffn_ple_final" if final else "ffn_ple",
    )(x, p, fg.reshape(1, D), w_gate_up, w_down, pn.reshape(1, D), wpg, wple, fn.reshape(1, D))


def _block_diag(w):
    H, I, J = w.shape
    eye = jnp.eye(H, dtype=w.dtype)
    return (eye[:, None, :, None] * w[:, :, None, :]).reshape(H * I, H * J)


def _head_slots(w, width):
    K = w.shape[0]
    w = w.reshape(K, MLA_HEADS, width)
    return jnp.pad(w, ((0, 0), (0, 0), (0, HEAD_SLOT - width))).reshape(K, MLA_HEADS * HEAD_SLOT)


def _rope_tables(S):
    pos = jnp.arange(S, dtype=F32)
    inv_freq = ROPE_BASE ** (-jnp.arange(0, MLA_ROPE, 2, dtype=F32) / MLA_ROPE)
    ang = pos[:, None] * inv_freq[None, :]
    cos, sin = jnp.cos(ang), jnp.sin(ang)
    half = MLA_ROPE // 2
    z = lambda n: jnp.zeros((S, n), F32)
    tail = LANES - ROPE_LANE0 - MLA_ROPE
    cq_t = jnp.concatenate([jnp.ones((S, ROPE_LANE0), F32), cos, cos, z(tail)], axis=1)
    ck_t = jnp.concatenate([z(ROPE_LANE0), cos, cos, z(tail)], axis=1)
    s1 = jnp.concatenate([z(ROPE_LANE0), -sin, z(half), z(tail)], axis=1)
    s2 = jnp.concatenate([z(ROPE_LANE0), z(half), sin, z(tail)], axis=1)
    return cq_t, ck_t, s1, s2


def kernel(x, p, mix_norm, w_in, gate_b, conv_w, conv_b, lru_wa, lru_ba, lru_wx, lru_bx, lru_lambda,
           mla_q_norm, mla_wuq, mla_kv_norm, mla_wukv, fox_bf, w_br_a, w_br_b, w_br_c, w_o,
           ffn_norm, w_gate_up, w_down, ple_norm, w_ple_gate, w_ple, final_norm):
    B, S, D = x.shape
    depth = w_in.shape[0]
    T = B * S
    tabs = _rope_tables(S)
    bias_tabs = _fox_bias_tables()
    mla_scale = (MLA_NOPE + MLA_ROPE) ** -0.5 * LOG2E
    fox_scale = FOX_HEAD_DIM ** -0.5 * LOG2E

    o_u = 0
    o_cq = 2 * LRU_WIDTH
    o_ckv = o_cq + MLA_Q_LORA
    o_kr = o_ckv + MLA_KV_LORA
    o_fq = o_kr + MLA_ROPE
    o_fl = o_fq + 3 * FOX_WIDTH
    o_g = o_fl + FOX_HEADS

    xf = x.reshape(T, D)
    for i in range(depth):
        wi = w_in[i]
        zcols = lambda n: jnp.zeros((D, n), F32)
        lat_w = MLA_Q_LORA + MLA_KV_LORA + LANES
        lat_pad = -lat_w % IN_TN
        nslot = FOX_HEADS * HEAD_SLOT
        w_all = jnp.concatenate([
            wi[:, o_u:o_cq],
            wi[:, o_cq:o_kr], zcols(ROPE_LANE0), wi[:, o_kr:o_fq], wi[:, o_fl:o_g],
            zcols(LANES - ROPE_LANE0 - MLA_ROPE - FOX_HEADS + lat_pad),
            _head_slots(wi[:, o_fq:o_fq + FOX_WIDTH], FOX_HEAD_DIM),
            _head_slots(wi[:, o_fq + FOX_WIDTH:o_fq + 2 * FOX_WIDTH], FOX_HEAD_DIM),
            wi[:, o_fq + 2 * FOX_WIDTH:o_fl],
            wi[:, o_g:]], axis=1).astype(BF16)
        widths = (2 * LRU_WIDTH, lat_w + lat_pad, 2 * nslot + FOX_WIDTH, 3 * D)
        n_before_fox, n_before_gate = widths[0] + widths[1], widths[0] + widths[1] + widths[2]
        cs = jnp.ones((1, sum(widths)), F32).at[0, n_before_fox:n_before_fox + nslot].set(fox_scale)
        cb = jnp.zeros((1, sum(widths)), F32).at[0, n_before_gate:].set(gate_b[i])
        u, lat, fox, g = _in_proj(xf, mix_norm[i], w_all, cs, cb, widths)

        ya = _lru_branch(u, conv_w[i], conv_b[i], _block_diag(lru_wa[i]).astype(BF16), lru_ba[i],
                         _block_diag(lru_wx[i]).astype(BF16), lru_bx[i], lru_lambda[i], B, S)

        wq = _head_slots(mla_wuq[i], MLA_NOPE + MLA_ROPE).astype(BF16)
        wukv = mla_wukv[i].reshape(MLA_KV_LORA, MLA_HEADS, MLA_NOPE + MLA_V)
        wk = _head_slots(wukv[:, :, :MLA_NOPE].reshape(MLA_KV_LORA, MLA_HEADS * MLA_NOPE), MLA_NOPE).astype(BF16)
        wv = wukv[:, :, MLA_NOPE:].reshape(MLA_KV_LORA, MLA_HEADS * MLA_V).astype(BF16)
        bf_row = jnp.zeros((1, LANES), F32).at[0, FLOGIT_LANE0:FLOGIT_LANE0 + FOX_HEADS].set(fox_bf[i])
        q_m, k_m, vt_m, q_f, k_f, vt_f = _prep(lat, fox, tabs, mla_q_norm[i], mla_kv_norm[i], wq, wk, wv, bf_row,
                                               bias_tabs, B, S, mla_scale)

        ob = _attention(q_m, k_m, vt_m, B, S, unit=MLA_CHUNK, name="mla_attention")
        oc = _attention(q_f, k_f, vt_f, B, S, unit=1, name="fox_attention")

        xf = _merge(ya.reshape(T, -1), ob.reshape(T, -1), oc.reshape(T, -1), g, xf,
                    w_br_a[i].astype(BF16), w_br_b[i].astype(BF16), w_br_c[i].astype(BF16), w_o[i].astype(BF16))
        xf = _ffn_ple(xf, p[i].reshape(T, -1), ffn_norm[i], w_gate_up[i].astype(BF16), w_down[i].astype(BF16),
                      ple_norm[i], w_ple_gate[i].astype(BF16), w_ple[i].astype(BF16), final_norm,
                      final=(i == depth - 1))
    return xf.reshape(B, S, D)
```

```python
import functools
import math

import jax
import jax.numpy as jnp
from jax import lax
from jax.experimental import pallas as pl
from jax.experimental.pallas import tpu as pltpu

F32 = jnp.float32
BF16 = jnp.bfloat16

EPS = 1e-6
NEG_INF = -1e30
LOG2E = math.log2(math.e)

LRU_WIDTH = 512
LRU_HEADS = 8
CONV_WIDTH = 4
LRU_C = 8.0

MLA_HEADS = 8
MLA_Q_LORA = 384
MLA_KV_LORA = 256
MLA_NOPE = 64
MLA_ROPE = 32
MLA_V = 64
ROPE_BASE = 10000.0
MLA_CHUNK = 64

FOX_HEADS = 8
FOX_HEAD_DIM = 64
FOX_WIDTH = FOX_HEADS * FOX_HEAD_DIM

LANES = 128
SUBLANES = 8
HEAD_SLOT = LANES
ROPE_LANE0 = MLA_NOPE
FLOGIT_LANE0 = MLA_NOPE + MLA_ROPE
V_ROWS = 80
Q_SUB = 256
SKEW = 4
LOOP_CHUNKS = 4

VMEM_LIMIT = 56 * 1024 * 1024


def _cparams(sem):
    return pltpu.CompilerParams(dimension_semantics=sem, vmem_limit_bytes=VMEM_LIMIT)


def _rms(xf, g):
    ms = jnp.mean(xf * xf, axis=-1, keepdims=True)
    return xf * lax.rsqrt(ms + EPS) * g


def _sigmoid(z):
    return 0.5 * jnp.tanh(0.5 * z) + 0.5


def _softplus(z):
    return jnp.maximum(z, 0.0) + jnp.log1p(jnp.exp(-jnp.abs(z)))


IN_TN = 256


def _in_proj_kernel(x_ref, g_ref, w_ref, cs_ref, cb_ref, u_ref, lat_ref, fox_ref, gate_ref, *, widths):
    h = _rms(x_ref[...], g_ref[...]).astype(BF16)
    outs = (u_ref, lat_ref, fox_ref, gate_ref)
    col = 0
    for slab, (o_ref, width) in enumerate(zip(outs, widths)):
        for j in range(width // IN_TN):
            sl = slice(col + j * IN_TN, col + (j + 1) * IN_TN)
            acc = jnp.dot(h, w_ref[:, sl], preferred_element_type=F32)
            if slab == 2:
                acc = acc * cs_ref[:, sl]
            if slab == 3:
                acc = _sigmoid(acc + cb_ref[:, sl])
            o_ref[:, j * IN_TN:(j + 1) * IN_TN] = acc.astype(o_ref.dtype)
        col += width


def _resident(shape):
    return pl.BlockSpec(shape, lambda i: (0, 0), pipeline_mode=pl.Buffered(1))


def _in_proj(x, g, w_all, cs, cb, widths, tm=512):
    T, D = x.shape
    N = sum(widths)
    return pl.pallas_call(
        functools.partial(_in_proj_kernel, widths=widths),
        grid=(T // tm,),
        in_specs=[
            pl.BlockSpec((tm, D), lambda i: (i, 0)),
            _resident((1, D)), _resident((D, N)), _resident((1, N)), _resident((1, N)),
        ],
        out_specs=[pl.BlockSpec((tm, w), lambda i: (i, 0)) for w in widths],
        out_shape=[jax.ShapeDtypeStruct((T, widths[0]), BF16), jax.ShapeDtypeStruct((T, widths[1]), F32),
                   jax.ShapeDtypeStruct((T, widths[2]), BF16), jax.ShapeDtypeStruct((T, widths[3]), BF16)],
        compiler_params=_cparams(("parallel",)),
        name="in_proj",
    )(x, g.reshape(1, D), w_all, cs, cb)


def _lru_kernel(u_ref, cw_ref, cb_ref, wa_ref, ba_ref, wx_ref, bx_ref, lam_ref, o_ref, ext_scr, hc_scr, a_scr,
                b_scr, *, ts):
    W = LRU_WIDTH
    pad = 8

    @pl.when(pl.program_id(1) == 0)
    def _():
        ext_scr[0:pad, :] = jnp.zeros((pad, W), F32)
        hc_scr[...] = jnp.zeros((1, W), F32)

    u = u_ref[:, 0:W].astype(F32)
    ug = u_ref[:, W:2 * W].astype(F32)
    ext_scr[pad:pad + ts, :] = u
    xc = cb_ref[...] + ext_scr[pl.ds(pad - 3, ts), :] * cw_ref[0:1, :]
    xc = xc + ext_scr[pl.ds(pad - 2, ts), :] * cw_ref[1:2, :]
    xc = xc + ext_scr[pl.ds(pad - 1, ts), :] * cw_ref[2:3, :]
    xc = xc + u * cw_ref[3:4, :]
    ext_scr[0:pad, :] = u[ts - pad:ts, :]

    xb = xc.astype(BF16)
    r = _sigmoid(jnp.dot(xb, wa_ref[...], preferred_element_type=F32) + ba_ref[...])
    ig = _sigmoid(jnp.dot(xb, wx_ref[...], preferred_element_type=F32) + bx_ref[...])
    log_a = (-LRU_C) * r * _softplus(-lam_ref[...])
    a = jnp.exp(log_a)
    b = jnp.sqrt(jnp.tanh(-log_a) * (a * a + 1.0)) * (ig * xc)

    a_scr[...] = a
    b_scr[...] = b
    row = lax.broadcasted_iota(jnp.int32, (SUBLANES, W), 0)

    def block(i, h_prev):
        r0 = pl.multiple_of(i * SUBLANES, SUBLANES)
        aa = a_scr[pl.ds(r0, SUBLANES), :]
        bb = b_scr[pl.ds(r0, SUBLANES), :]
        d = 1
        while d < SUBLANES:
            keep = row >= d
            a_sh = jnp.where(keep, pltpu.roll(aa, d, 0), 1.0)
            b_sh = jnp.where(keep, pltpu.roll(bb, d, 0), 0.0)
            bb = aa * b_sh + bb
            aa = aa * a_sh
            d *= 2
        h = bb + aa * h_prev
        b_scr[pl.ds(r0, SUBLANES), :] = h
        return h[SUBLANES - 1:SUBLANES, :]

    hc_scr[...] = lax.fori_loop(0, ts // SUBLANES, block, hc_scr[...], unroll=4)
    o_ref[...] = (b_scr[...] * jax.nn.gelu(ug)).astype(o_ref.dtype)


def _lru_branch(u, conv_w, conv_b, wa_bd, ba, wx_bd, bx, lam, B, S, ts=256):
    W = LRU_WIDTH
    row = lambda v: v.reshape(1, W)
    full = lambda shape: pl.BlockSpec(shape, lambda b, t: (0,) * len(shape))
    return pl.pallas_call(
        functools.partial(_lru_kernel, ts=ts),
        grid=(B, S // ts),
        in_specs=[
            pl.BlockSpec((None, ts, 2 * W), lambda b, t: (b, t, 0)),
            full((CONV_WIDTH, W)), full((1, W)),
            full((W, W)), full((1, W)), full((W, W)), full((1, W)), full((1, W)),
        ],
        out_specs=pl.BlockSpec((None, ts, W), lambda b, t: (b, t, 0)),
        out_shape=jax.ShapeDtypeStruct((B, S, W), BF16),
        scratch_shapes=[pltpu.VMEM((ts + 8, W), F32), pltpu.VMEM((1, W), F32), pltpu.VMEM((ts, W), F32),
                        pltpu.VMEM((ts, W), F32)],
        compiler_params=_cparams(("parallel", "arbitrary")),
        name="lru_scan",
    )(u.reshape(B, S, 2 * W), conv_w, row(conv_b), wa_bd, row(ba), wx_bd, row(bx), row(lam))


def _rope_slot(x, c, s1, s2):
    return x * c + pltpu.roll(x, LANES - MLA_ROPE // 2, 1) * s1 + pltpu.roll(x, MLA_ROPE // 2, 1) * s2


def _bf16_terms(c):
    hi = c.astype(BF16).astype(F32)
    r = c - hi
    lo = r.astype(BF16).astype(F32)
    return hi, lo, (r - lo).astype(BF16).astype(F32)


def _fox_bias_tables():
    rows = jnp.arange(3 * LANES)
    term, head = rows // LANES, rows % LANES - FLOGIT_LANE0
    valid = (head >= 0) & (head < FOX_HEADS)
    cols = jnp.arange(FOX_HEADS * HEAD_SLOT)
    col_head, col_lane = cols // HEAD_SLOT, cols % HEAD_SLOT - FOX_HEAD_DIM
    hit = valid[:, None] & (head[:, None] == col_head[None, :])
    sel_q = jnp.where(hit & (col_lane[None, :] == term[:, None]), 1.0, 0.0).astype(BF16)
    sel_k = jnp.where(hit & (col_lane[None, :] == term[:, None] + 3), -1.0, 0.0).astype(BF16)
    one_q = jnp.where((col_lane >= 3) & (col_lane < 6), 1.0, 0.0).astype(F32)[None, :]
    one_k = jnp.where((col_lane >= 0) & (col_lane < 3), 1.0, 0.0).astype(F32)[None, :]
    return sel_q, sel_k, one_q, one_k


def _store_values_t(vt_ref, v, ts):
    vt = v.T
    row = lax.broadcasted_iota(jnp.int32, (V_ROWS - MLA_V, ts), 0)
    ones_tile = jnp.where(row == 0, 1.0, 0.0).astype(vt_ref.dtype)
    for h in range(MLA_HEADS):
        vt_ref[h * V_ROWS:h * V_ROWS + MLA_V, :] = vt[h * MLA_V:(h + 1) * MLA_V, :].astype(vt_ref.dtype)
        vt_ref[h * V_ROWS + MLA_V:(h + 1) * V_ROWS, :] = ones_tile


def _prep_kernel(lat_ref, fox_ref, cq_t_ref, ck_t_ref, s1_ref, s2_ref, qn_ref, kvn_ref, wq_ref, wk_ref, wv_ref,
                 bf_ref, selq_ref, selk_ref, oneq_ref, onek_ref, qm_ref, km_ref, vtm_ref, qf_ref, kf_ref, vtf_ref,
                 carry_scr, *, ts, q_scale):
    @pl.when(pl.program_id(1) == 0)
    def _():
        carry_scr[...] = jnp.zeros((1, LANES), F32)

    cq_t, ck_t, s1, s2 = cq_t_ref[...], ck_t_ref[...], s1_ref[...], s2_ref[...]

    cqn = _rms(lat_ref[:, 0:MLA_Q_LORA], qn_ref[...]).astype(BF16)
    q = jnp.dot(cqn, wq_ref[...], preferred_element_type=F32)
    for h in range(MLA_HEADS):
        sl = slice(h * HEAD_SLOT, (h + 1) * HEAD_SLOT)
        qm_ref[:, sl] = (_rope_slot(q[:, sl], cq_t, s1, s2) * q_scale).astype(qm_ref.dtype)

    ckvn = _rms(lat_ref[:, MLA_Q_LORA:MLA_Q_LORA + MLA_KV_LORA], kvn_ref[...]).astype(BF16)
    kn = jnp.dot(ckvn, wk_ref[...], preferred_element_type=F32)
    misc = lat_ref[:, MLA_Q_LORA + MLA_KV_LORA:MLA_Q_LORA + MLA_KV_LORA + LANES]
    kr = _rope_slot(misc, ck_t, s1, s2)
    for h in range(MLA_HEADS):
        sl = slice(h * HEAD_SLOT, (h + 1) * HEAD_SLOT)
        km_ref[:, sl] = (kn[:, sl] + kr).astype(km_ref.dtype)
    _store_values_t(vtm_ref, jnp.dot(ckvn, wv_ref[...], preferred_element_type=F32), ts)

    c = -_softplus(-(misc + bf_ref[...]))
    row = lax.broadcasted_iota(jnp.int32, (ts, LANES), 0)
    d = 1
    while d < ts:
        c = c + jnp.where(row >= d, pltpu.roll(c, d, 0), 0.0)
        d *= 2
    c = c + carry_scr[...]
    carry_scr[...] = c[ts - 1:ts, :]
    c = c * LOG2E

    terms = jnp.concatenate(_bf16_terms(c), axis=1).astype(BF16)
    nslot = FOX_HEADS * HEAD_SLOT
    q_bias = jnp.dot(terms, selq_ref[...], preferred_element_type=F32) + oneq_ref[...]
    k_bias = jnp.dot(terms, selk_ref[...], preferred_element_type=F32) + onek_ref[...]
    qf_ref[...] = (fox_ref[:, 0:nslot].astype(F32) + q_bias).astype(qf_ref.dtype)
    kf_ref[...] = (fox_ref[:, nslot:2 * nslot].astype(F32) + k_bias).astype(kf_ref.dtype)
    _store_values_t(vtf_ref, fox_ref[:, 2 * nslot:2 * nslot + FOX_WIDTH].astype(F32), ts)


def _prep(lat, fox, tabs, qn, kvn, wq, wk, wv, bf_row, bias_tabs, B, S, q_scale, ts=512):
    NL, NF = lat.shape[-1], fox.shape[-1]
    full = lambda shape: pl.BlockSpec(shape, lambda b, t: (0,) * len(shape))
    tab = pl.BlockSpec((ts, LANES), lambda b, t: (t, 0))
    HS = MLA_HEADS * HEAD_SLOT
    slots = pl.BlockSpec((None, ts, HS), lambda b, t: (b, t, 0))
    vts = pl.BlockSpec((None, MLA_HEADS * V_ROWS, ts), lambda b, t: (b, 0, t))
    slot_shape = jax.ShapeDtypeStruct((B, S, HS), BF16)
    vt_shape = jax.ShapeDtypeStruct((B, MLA_HEADS * V_ROWS, S), BF16)
    return pl.pallas_call(
        functools.partial(_prep_kernel, ts=ts, q_scale=q_scale),
        grid=(B, S // ts),
        in_specs=[
            pl.BlockSpec((None, ts, NL), lambda b, t: (b, t, 0)),
            pl.BlockSpec((None, ts, NF), lambda b, t: (b, t, 0)),
            tab, tab, tab, tab,
            full((1, MLA_Q_LORA)), full((1, MLA_KV_LORA)),
            full((MLA_Q_LORA, HS)), full((MLA_KV_LORA, HS)), full((MLA_KV_LORA, MLA_HEADS * MLA_V)),
            full((1, LANES)),
            full((3 * LANES, HS)), full((3 * LANES, HS)), full((1, HS)), full((1, HS)),
        ],
        out_specs=[slots, slots, vts, slots, slots, vts],
        out_shape=[slot_shape, slot_shape, vt_shape, slot_shape, slot_shape, vt_shape],
        scratch_shapes=[pltpu.VMEM((1, LANES), F32)],
        compiler_params=_cparams(("parallel", "arbitrary")),
        name="mla_fox_prep",
    )(lat.reshape(B, S, NL), fox.reshape(B, S, NF), *tabs, qn.reshape(1, -1), kvn.reshape(1, -1), wq, wk, wv,
      bf_row, *bias_tabs)


def _attn_kernel(q_ref, k_ref, vt_ref, o_ref, m_scr, acc_scr, s_scr, *, tq, tk, unit):
    qi = pl.program_id(2)
    nsub = tq // Q_SUB
    ndiag = tq // tk
    m_scr[...] = jnp.full(m_scr.shape, NEG_INF, F32)
    acc_scr[...] = jnp.zeros(acc_scr.shape, F32)

    def diag_kinds(d):
        kinds = []
        for c in range(nsub):
            if (c + 1) * Q_SUB <= d * tk:
                kinds.append(None)
            elif (d + 1) * tk <= c * Q_SUB:
                kinds.append((tk, None))
            else:
                kinds.append((min(tk, (c + 1) * Q_SUB - d * tk), d * tk))
        return kinds

    def units(k0, kinds, slot0):
        live = [(h, c) for c in range(nsub) for h in range(2) if kinds[c] is not None]
        return [(k0, kinds[c][0], kinds[c][1], h, c, slot0 + idx) for idx, (h, c) in enumerate(live)]

    def logits_one(k0, klen, h, c, slot):
        kh = k_ref[pl.ds(k0, klen), h * HEAD_SLOT:(h + 1) * HEAD_SLOT]
        qh = q_ref[c * Q_SUB:(c + 1) * Q_SUB, h * HEAD_SLOT:(h + 1) * HEAD_SLOT]
        s_scr[slot, 0:klen] = lax.dot_general(kh, qh, (((1,), (1,)), ((), ())), preferred_element_type=F32)

    def softmax_one(k0, klen, mask_off, h, c, slot):
        st = s_scr[slot, 0:klen]
        cs = slice(c * Q_SUB, (c + 1) * Q_SUB)
        if mask_off is not None:
            kk = lax.broadcasted_iota(jnp.int32, (klen, Q_SUB), 0) + mask_off
            qq = lax.broadcasted_iota(jnp.int32, (klen, Q_SUB), 1) + c * Q_SUB
            if unit > 1:
                sh = unit.bit_length() - 1
                kk, qq = kk >> sh, qq >> sh
            st = jnp.where(qq >= kk, st, NEG_INF)
        m_prev = m_scr[h, :, cs]
        m_new = jnp.maximum(m_prev, jnp.max(st, axis=0, keepdims=True))
        alpha = jnp.exp2(m_prev - m_new)
        pt = jnp.exp2(st - m_new).astype(BF16)
        m_scr[h, :, cs] = m_new
        vth = vt_ref[h * V_ROWS:(h + 1) * V_ROWS, pl.ds(k0, klen)]
        acc_scr[h, :, cs] = alpha * acc_scr[h, :, cs] + jnp.dot(vth, pt, preferred_element_type=F32)

    def run(todo, ahead):
        for j in range(len(todo)):
            if j < len(ahead):
                k0, klen, _, h, c, slot = ahead[j]
                logits_one(k0, klen, h, c, slot)
            softmax_one(*todo[j])

    n = qi * ndiag
    per = 2 * nsub
    all_visible = [(tk, None)] * nsub
    for k0, klen, _, h, c, slot in units(0, all_visible, 0)[:SKEW]:
        logits_one(k0, klen, h, c, slot)

    def body(i, carry):
        chunks = [units(pl.multiple_of((LOOP_CHUNKS * i + t) * tk, tk), all_visible, (t % 2) * per)
                  for t in range(LOOP_CHUNKS + 1)]
        flat = sum(chunks, [])
        run(flat[:LOOP_CHUNKS * per], flat[SKEW:SKEW + LOOP_CHUNKS * per])
        return carry

    lax.fori_loop(0, n // LOOP_CHUNKS, body, 0)
    flat = []
    for d in range(ndiag):
        flat += units(pl.multiple_of(qi * tq + d * tk, tk), diag_kinds(d), (d % 2) * per)
    run(flat, flat[SKEW:])

    outs = []
    for h in range(2):
        a = acc_scr[h]
        outs.append(a[0:MLA_V, :] / a[MLA_V:MLA_V + 1, :])
    o_ref[...] = jnp.concatenate(outs, axis=0).T.astype(o_ref.dtype)


def _attention(q, k, vt, B, S, *, unit, name, tq=2048, tk=512):
    npair = MLA_HEADS // 2
    nstream = 2 * (tq // Q_SUB)
    assert (tq // tk) % LOOP_CHUNKS == 0 and LOOP_CHUNKS % 2 == 0
    return pl.pallas_call(
        functools.partial(_attn_kernel, tq=tq, tk=tk, unit=unit),
        grid=(B, npair, S // tq),
        in_specs=[
            pl.BlockSpec((None, tq, 2 * HEAD_SLOT), lambda b, p, i: (b, i, p)),
            pl.BlockSpec((None, S, 2 * HEAD_SLOT), lambda b, p, i: (b, 0, p)),
            pl.BlockSpec((None, 2 * V_ROWS, S), lambda b, p, i: (b, p, 0)),
        ],
        out_specs=pl.BlockSpec((None, tq, 2 * MLA_V), lambda b, p, i: (b, i, p)),
        out_shape=jax.ShapeDtypeStruct((B, S, npair * 2 * MLA_V), BF16),
        scratch_shapes=[pltpu.VMEM((2, 1, tq), F32), pltpu.VMEM((2, V_ROWS, tq), F32),
                        pltpu.VMEM((2 * nstream, tk, Q_SUB), F32)],
        compiler_params=_cparams(("parallel", "parallel", "arbitrary")),
        name=name,
    )(q, k, vt)


def _merge_kernel(ya_ref, ob_ref, oc_ref, g_ref, x_ref, wa_ref, wb_ref, wc_ref, wo_ref, o_ref):
    D = x_ref.shape[-1]
    ya = jnp.dot(ya_ref[...], wa_ref[...], preferred_element_type=F32)
    yb = jnp.dot(ob_ref[...], wb_ref[...], preferred_element_type=F32)
    yc = jnp.dot(oc_ref[...], wc_ref[...], preferred_element_type=F32)
    merged = g_ref[:, 0:D] * ya + g_ref[:, D:2 * D] * yb + g_ref[:, 2 * D:3 * D] * yc
    o_ref[...] = x_ref[...] + jnp.dot(merged.astype(BF16), wo_ref[...], preferred_element_type=F32)


def _merge(ya, ob, oc, g, x, wa, wb, wc, wo, tm=512):
    T, D = x.shape
    W = ya.shape[-1]
    rows = lambda n: pl.BlockSpec((tm, n), lambda i: (i, 0))
    return pl.pallas_call(
        _merge_kernel,
        grid=(T // tm,),
        in_specs=[rows(W), rows(W), rows(W), rows(3 * D), rows(D),
                  _resident((W, D)), _resident((W, D)), _resident((W, D)), _resident((D, D))],
        out_specs=rows(D),
        out_shape=jax.ShapeDtypeStruct((T, D), F32),
        compiler_params=_cparams(("parallel",)),
        name="merge",
    )(ya, ob, oc, g, x, wa, wb, wc, wo)


FFN_TF = 1408


def _ffn_ple_kernel(x_ref, p_ref, fg_ref, wgu_ref, wd_ref, pn_ref, wpg_ref, wple_ref, fn_ref, o_ref, *, final):
    d_ff = wd_ref.shape[0]
    x = x_ref[...]
    h = _rms(x, fg_ref[...]).astype(BF16)
    for c in range(d_ff // FFN_TF):
        gate = jnp.dot(h, wgu_ref[:, c * FFN_TF:(c + 1) * FFN_TF], preferred_element_type=F32)
        up = jnp.dot(h, wgu_ref[:, d_ff + c * FFN_TF:d_ff + (c + 1) * FFN_TF], preferred_element_type=F32)
        act = (gate * _sigmoid(gate) * up).astype(BF16)
        x = x + jnp.dot(act, wd_ref[c * FFN_TF:(c + 1) * FFN_TF, :], preferred_element_type=F32)
    hn = _rms(x, pn_ref[...]).astype(BF16)
    pg = _sigmoid(jnp.dot(hn, wpg_ref[...], preferred_element_type=F32))
    pe = jnp.dot(p_ref[...].astype(BF16), wple_ref[...], preferred_element_type=F32)
    x = x + pg * pe
    if final:
        x = _rms(x, fn_ref[...])
    o_ref[...] = x


def _ffn_ple(x, p, fg, w_gate_up, w_down, pn, wpg, wple, fn, final, tm=512):
    T, D = x.shape
    P = p.shape[-1]
    d_ff = w_down.shape[0]
    assert d_ff % FFN_TF == 0
    return pl.pallas_call(
        functools.partial(_ffn_ple_kernel, final=final),
        grid=(T // tm,),
        in_specs=[pl.BlockSpec((tm, D), lambda i: (i, 0)), pl.BlockSpec((tm, P), lambda i: (i, 0)),
                  _resident((1, D)), _resident((D, 2 * d_ff)), _resident((d_ff, D)),
                  _resident((1, D)), _resident((D, D)), _resident((P, D)), _resident((1, D))],
        out_specs=pl.BlockSpec((tm, D), lambda i: (i, 0)),
        out_shape=jax.ShapeDtypeStruct((T, D), F32),
        compiler_params=_cparams(("parallel",)),
        name="ffn_ple_final" if final else "ffn_ple",
    )(x, p, fg.reshape(1, D), w_gate_up, w_down, pn.reshape(1, D), wpg, wple, fn.reshape(1, D))


def _block_diag(w):
    L, H, I, J = w.shape
    eye = jnp.eye(H, dtype=w.dtype)
    return (eye[None, :, None, :, None] * w[:, :, :, None, :]).reshape(L, H * I, H * J)


def _head_slots(w, width):
    lead = w.shape[:-1]
    w = w.reshape(*lead, MLA_HEADS, width)
    pad = [(0, 0)] * (w.ndim - 1) + [(0, HEAD_SLOT - width)]
    return jnp.pad(w, pad).reshape(*lead, MLA_HEADS * HEAD_SLOT)


def _rope_tables(S):
    pos = jnp.arange(S, dtype=F32)
    inv_freq = ROPE_BASE ** (-jnp.arange(0, MLA_ROPE, 2, dtype=F32) / MLA_ROPE)
    ang = pos[:, None] * inv_freq[None, :]
    cos, sin = jnp.cos(ang), jnp.sin(ang)
    half = MLA_ROPE // 2
    z = lambda n: jnp.zeros((S, n), F32)
    tail = LANES - ROPE_LANE0 - MLA_ROPE
    cq_t = jnp.concatenate([jnp.ones((S, ROPE_LANE0), F32), cos, cos, z(tail)], axis=1)
    ck_t = jnp.concatenate([z(ROPE_LANE0), cos, cos, z(tail)], axis=1)
    s1 = jnp.concatenate([z(ROPE_LANE0), -sin, z(half), z(tail)], axis=1)
    s2 = jnp.concatenate([z(ROPE_LANE0), z(half), sin, z(tail)], axis=1)
    return cq_t, ck_t, s1, s2


def kernel(x, p, mix_norm, w_in, gate_b, conv_w, conv_b, lru_wa, lru_ba, lru_wx, lru_bx, lru_lambda,
           mla_q_norm, mla_wuq, mla_kv_norm, mla_wukv, fox_bf, w_br_a, w_br_b, w_br_c, w_o,
           ffn_norm, w_gate_up, w_down, ple_norm, w_ple_gate, w_ple, final_norm):
    B, S, D = x.shape
    depth = w_in.shape[0]
    T = B * S
    tabs = _rope_tables(S)
    bias_tabs = _fox_bias_tables()
    mla_scale = (MLA_NOPE + MLA_ROPE) ** -0.5 * LOG2E
    fox_scale = FOX_HEAD_DIM ** -0.5 * LOG2E

    o_u = 0
    o_cq = 2 * LRU_WIDTH
    o_ckv = o_cq + MLA_Q_LORA
    o_kr = o_ckv + MLA_KV_LORA
    o_fq = o_kr + MLA_ROPE
    o_fl = o_fq + 3 * FOX_WIDTH
    o_g = o_fl + FOX_HEADS

    wb = w_in.astype(BF16)
    zcols = lambda n: jnp.zeros((depth, D, n), BF16)
    lat_w = MLA_Q_LORA + MLA_KV_LORA + LANES
    lat_pad = -lat_w % IN_TN
    nslot = FOX_HEADS * HEAD_SLOT
    w_all = jnp.concatenate([
        wb[..., o_u:o_cq],
        wb[..., o_cq:o_kr], zcols(ROPE_LANE0), wb[..., o_kr:o_fq], wb[..., o_fl:o_g],
        zcols(LANES - ROPE_LANE0 - MLA_ROPE - FOX_HEADS + lat_pad),
        _head_slots(wb[..., o_fq:o_fq + FOX_WIDTH], FOX_HEAD_DIM),
        _head_slots(wb[..., o_fq + FOX_WIDTH:o_fq + 2 * FOX_WIDTH], FOX_HEAD_DIM),
        wb[..., o_fq + 2 * FOX_WIDTH:o_fl],
        wb[..., o_g:]], axis=-1)
    widths = (2 * LRU_WIDTH, lat_w + lat_pad, 2 * nslot + FOX_WIDTH, 3 * D)
    n_before_fox, n_before_gate = widths[0] + widths[1], widths[0] + widths[1] + widths[2]
    cs = jnp.ones((1, sum(widths)), F32).at[0, n_before_fox:n_before_fox + nslot].set(fox_scale)
    cb = jnp.concatenate([jnp.zeros((depth, n_before_gate), F32), gate_b], axis=1)[:, None, :]
    wa_bd, wx_bd = _block_diag(lru_wa).astype(BF16), _block_diag(lru_wx).astype(BF16)
    wq = _head_slots(mla_wuq.astype(BF16), MLA_NOPE + MLA_ROPE)
    wukv = mla_wukv.astype(BF16).reshape(depth, MLA_KV_LORA, MLA_HEADS, MLA_NOPE + MLA_V)
    wk = _head_slots(wukv[..., :MLA_NOPE].reshape(depth, MLA_KV_LORA, MLA_HEADS * MLA_NOPE), MLA_NOPE)
    wv = wukv[..., MLA_NOPE:].reshape(depth, MLA_KV_LORA, MLA_HEADS * MLA_V)
    bf_rows = jnp.zeros((depth, 1, LANES), F32).at[:, 0, FLOGIT_LANE0:FLOGIT_LANE0 + FOX_HEADS].set(fox_bf)
    wbr_a, wbr_b, wbr_c, wo = (w.astype(BF16) for w in (w_br_a, w_br_b, w_br_c, w_o))
    wgu, wdn, wpg, wpl = (w.astype(BF16) for w in (w_gate_up, w_down, w_ple_gate, w_ple))

    xf = x.reshape(T, D)
    for i in range(depth):
        u, lat, fox, g = _in_proj(xf, mix_norm[i], w_all[i], cs, cb[i], widths)
        ya = _lru_branch(u, conv_w[i], conv_b[i], wa_bd[i], lru_ba[i], wx_bd[i], lru_bx[i], lru_lambda[i], B, S)
        q_m, k_m, vt_m, q_f, k_f, vt_f = _prep(lat, fox, tabs, mla_q_norm[i], mla_kv_norm[i], wq[i], wk[i], wv[i],
                                               bf_rows[i], bias_tabs, B, S, mla_scale)
        ob = _attention(q_m, k_m, vt_m, B, S, unit=MLA_CHUNK, name="mla_attention")
        oc = _attention(q_f, k_f, vt_f, B, S, unit=1, name="fox_attention")
        xf = _merge(ya.reshape(T, -1), ob.reshape(T, -1), oc.reshape(T, -1), g, xf,
                    wbr_a[i], wbr_b[i], wbr_c[i], wo[i])
        xf = _ffn_ple(xf, p[i].reshape(T, -1), ffn_norm[i], wgu[i], wdn[i], ple_norm[i], wpg[i], wpl[i],
                      final_norm, final=(i == depth - 1))
    return xf.reshape(B, S, D)
```

```python
import functools
import math

import jax
import jax.numpy as jnp
from jax import lax
from jax.experimental import pallas as pl
from jax.experimental.pallas import tpu as pltpu

F32 = jnp.float32
BF16 = jnp.bfloat16

EPS = 1e-6
NEG_INF = -1e30
LOG2E = math.log2(math.e)

LRU_WIDTH = 512
LRU_HEADS = 8
CONV_WIDTH = 4
LRU_C = 8.0

MLA_HEADS = 8
MLA_Q_LORA = 384
MLA_KV_LORA = 256
MLA_NOPE = 64
MLA_ROPE = 32
MLA_V = 64
ROPE_BASE = 10000.0
MLA_CHUNK = 64

FOX_HEADS = 8
FOX_HEAD_DIM = 64
FOX_WIDTH = FOX_HEADS * FOX_HEAD_DIM

LANES = 128
SUBLANES = 8
HEAD_SLOT = LANES
ROPE_LANE0 = MLA_NOPE
FLOGIT_LANE0 = MLA_NOPE + MLA_ROPE
V_ROWS = 80
Q_SUB = 256
SKEW = 4
LOOP_CHUNKS = 4

VMEM_LIMIT = 56 * 1024 * 1024


def _cparams(sem):
    return pltpu.CompilerParams(dimension_semantics=sem, vmem_limit_bytes=VMEM_LIMIT)


def _rms(xf, g):
    ms = jnp.mean(xf * xf, axis=-1, keepdims=True)
    return xf * lax.rsqrt(ms + EPS) * g


def _sigmoid(z):
    return 0.5 * jnp.tanh(0.5 * z) + 0.5


def _softplus(z):
    return jnp.maximum(z, 0.0) + jnp.log1p(jnp.exp(-jnp.abs(z)))


IN_TN = 256


def _in_proj_kernel(x_ref, g_ref, w_ref, cs_ref, cb_ref, u_ref, lat_ref, fox_ref, gate_ref, *, widths):
    h = _rms(x_ref[...], g_ref[...]).astype(BF16)
    outs = (u_ref, lat_ref, fox_ref, gate_ref)
    col = 0
    for slab, (o_ref, width) in enumerate(zip(outs, widths)):
        for j in range(width // IN_TN):
            sl = slice(col + j * IN_TN, col + (j + 1) * IN_TN)
            acc = jnp.dot(h, w_ref[:, sl], preferred_element_type=F32)
            if slab == 2:
                acc = acc * cs_ref[:, sl]
            if slab == 3:
                acc = _sigmoid(acc + cb_ref[:, sl])
            o_ref[:, j * IN_TN:(j + 1) * IN_TN] = acc.astype(o_ref.dtype)
        col += width


def _resident(shape):
    return pl.BlockSpec(shape, lambda i: (0, 0), pipeline_mode=pl.Buffered(1))


def _in_proj(x, g, w_all, cs, cb, widths, tm=512):
    T, D = x.shape
    N = sum(widths)
    return pl.pallas_call(
        functools.partial(_in_proj_kernel, widths=widths),
        grid=(T // tm,),
        in_specs=[
            pl.BlockSpec((tm, D), lambda i: (i, 0)),
            _resident((1, D)), _resident((D, N)), _resident((1, N)), _resident((1, N)),
        ],
        out_specs=[pl.BlockSpec((tm, w), lambda i: (i, 0)) for w in widths],
        out_shape=[jax.ShapeDtypeStruct((T, widths[0]), BF16), jax.ShapeDtypeStruct((T, widths[1]), F32),
                   jax.ShapeDtypeStruct((T, widths[2]), BF16), jax.ShapeDtypeStruct((T, widths[3]), BF16)],
        compiler_params=_cparams(("parallel",)),
        name="in_proj",
    )(x, g.reshape(1, D), w_all, cs, cb)


def _lru_kernel(u_ref, cw_ref, cb_ref, wa_ref, ba_ref, wx_ref, bx_ref, lam_ref, o_ref, ext_scr, hc_scr, a_scr,
                b_scr, *, ts):
    W = LRU_WIDTH
    pad = 8

    @pl.when(pl.program_id(1) == 0)
    def _():
        ext_scr[0:pad, :] = jnp.zeros((pad, W), F32)
        hc_scr[...] = jnp.zeros((1, W), F32)

    u = u_ref[:, 0:W].astype(F32)
    ug = u_ref[:, W:2 * W].astype(F32)
    ext_scr[pad:pad + ts, :] = u
    xc = cb_ref[...] + ext_scr[pl.ds(pad - 3, ts), :] * cw_ref[0:1, :]
    xc = xc + ext_scr[pl.ds(pad - 2, ts), :] * cw_ref[1:2, :]
    xc = xc + ext_scr[pl.ds(pad - 1, ts), :] * cw_ref[2:3, :]
    xc = xc + u * cw_ref[3:4, :]
    ext_scr[0:pad, :] = u[ts - pad:ts, :]

    xb = xc.astype(BF16)
    r = _sigmoid(jnp.dot(xb, wa_ref[...], preferred_element_type=F32) + ba_ref[...])
    ig = _sigmoid(jnp.dot(xb, wx_ref[...], preferred_element_type=F32) + bx_ref[...])
    log_a = (-LRU_C) * r * _softplus(-lam_ref[...])
    a = jnp.exp(log_a)
    b = jnp.sqrt(jnp.tanh(-log_a) * (a * a + 1.0)) * (ig * xc)

    a_scr[...] = a
    b_scr[...] = b
    row = lax.broadcasted_iota(jnp.int32, (SUBLANES, W), 0)

    def block(i, h_prev):
        r0 = pl.multiple_of(i * SUBLANES, SUBLANES)
        aa = a_scr[pl.ds(r0, SUBLANES), :]
        bb = b_scr[pl.ds(r0, SUBLANES), :]
        d = 1
        while d < SUBLANES:
            keep = row >= d
            a_sh = jnp.where(keep, pltpu.roll(aa, d, 0), 1.0)
            b_sh = jnp.where(keep, pltpu.roll(bb, d, 0), 0.0)
            bb = aa * b_sh + bb
            aa = aa * a_sh
            d *= 2
        h = bb + aa * h_prev
        b_scr[pl.ds(r0, SUBLANES), :] = h
        return h[SUBLANES - 1:SUBLANES, :]

    hc_scr[...] = lax.fori_loop(0, ts // SUBLANES, block, hc_scr[...], unroll=4)
    o_ref[...] = (b_scr[...] * jax.nn.gelu(ug)).astype(o_ref.dtype)


def _lru_branch(u, conv_w, conv_b, wa_bd, ba, wx_bd, bx, lam, B, S, ts=256):
    W = LRU_WIDTH
    row = lambda v: v.reshape(1, W)
    full = lambda shape: pl.BlockSpec(shape, lambda b, t: (0,) * len(shape))
    return pl.pallas_call(
        functools.partial(_lru_kernel, ts=ts),
        grid=(B, S // ts),
        in_specs=[
            pl.BlockSpec((None, ts, 2 * W), lambda b, t: (b, t, 0)),
            full((CONV_WIDTH, W)), full((1, W)),
            full((W, W)), full((1, W)), full((W, W)), full((1, W)), full((1, W)),
        ],
        out_specs=pl.BlockSpec((None, ts, W), lambda b, t: (b, t, 0)),
        out_shape=jax.ShapeDtypeStruct((B, S, W), BF16),
        scratch_shapes=[pltpu.VMEM((ts + 8, W), F32), pltpu.VMEM((1, W), F32), pltpu.VMEM((ts, W), F32),
                        pltpu.VMEM((ts, W), F32)],
        compiler_params=_cparams(("parallel", "arbitrary")),
        name="lru_scan",
    )(u.reshape(B, S, 2 * W), conv_w, row(conv_b), wa_bd, row(ba), wx_bd, row(bx), row(lam))


def _rope_slot(x, c, s1, s2):
    return x * c + pltpu.roll(x, LANES - MLA_ROPE // 2, 1) * s1 + pltpu.roll(x, MLA_ROPE // 2, 1) * s2


def _bf16_terms(c):
    hi = c.astype(BF16).astype(F32)
    r = c - hi
    lo = r.astype(BF16).astype(F32)
    return hi, lo, (r - lo).astype(BF16).astype(F32)


def _fox_bias_tables():
    rows = jnp.arange(3 * LANES)
    term, head = rows // LANES, rows % LANES - FLOGIT_LANE0
    valid = (head >= 0) & (head < FOX_HEADS)
    cols = jnp.arange(FOX_HEADS * HEAD_SLOT)
    col_head, col_lane = cols // HEAD_SLOT, cols % HEAD_SLOT - FOX_HEAD_DIM
    hit = valid[:, None] & (head[:, None] == col_head[None, :])
    sel_q = jnp.where(hit & (col_lane[None, :] == term[:, None]), 1.0, 0.0).astype(BF16)
    sel_k = jnp.where(hit & (col_lane[None, :] == term[:, None] + 3), -1.0, 0.0).astype(BF16)
    one_q = jnp.where((col_lane >= 3) & (col_lane < 6), 1.0, 0.0).astype(F32)[None, :]
    one_k = jnp.where((col_lane >= 0) & (col_lane < 3), 1.0, 0.0).astype(F32)[None, :]
    return sel_q, sel_k, one_q, one_k


def _store_values_t(vt_ref, v, ts):
    vt = v.T
    row = lax.broadcasted_iota(jnp.int32, (V_ROWS - MLA_V, ts), 0)
    ones_tile = jnp.where(row == 0, 1.0, 0.0).astype(vt_ref.dtype)
    for h in range(MLA_HEADS):
        vt_ref[h * V_ROWS:h * V_ROWS + MLA_V, :] = vt[h * MLA_V:(h + 1) * MLA_V, :].astype(vt_ref.dtype)
        vt_ref[h * V_ROWS + MLA_V:(h + 1) * V_ROWS, :] = ones_tile


def _prep_kernel(lat_ref, fox_ref, cq_t_ref, ck_t_ref, s1_ref, s2_ref, qn_ref, kvn_ref, wq_ref, wk_ref, wv_ref,
                 bf_ref, selq_ref, selk_ref, oneq_ref, onek_ref, qm_ref, km_ref, vtm_ref, qf_ref, kf_ref, vtf_ref,
                 carry_scr, *, ts, q_scale):
    @pl.when(pl.program_id(1) == 0)
    def _():
        carry_scr[...] = jnp.zeros((1, LANES), F32)

    cq_t, ck_t, s1, s2 = cq_t_ref[...], ck_t_ref[...], s1_ref[...], s2_ref[...]

    cqn = _rms(lat_ref[:, 0:MLA_Q_LORA], qn_ref[...]).astype(BF16)
    q = jnp.dot(cqn, wq_ref[...], preferred_element_type=F32)
    for h in range(MLA_HEADS):
        sl = slice(h * HEAD_SLOT, (h + 1) * HEAD_SLOT)
        qm_ref[:, sl] = (_rope_slot(q[:, sl], cq_t, s1, s2) * q_scale).astype(qm_ref.dtype)

    ckvn = _rms(lat_ref[:, MLA_Q_LORA:MLA_Q_LORA + MLA_KV_LORA], kvn_ref[...]).astype(BF16)
    kn = jnp.dot(ckvn, wk_ref[...], preferred_element_type=F32)
    misc = lat_ref[:, MLA_Q_LORA + MLA_KV_LORA:MLA_Q_LORA + MLA_KV_LORA + LANES]
    kr = _rope_slot(misc, ck_t, s1, s2)
    for h in range(MLA_HEADS):
        sl = slice(h * HEAD_SLOT, (h + 1) * HEAD_SLOT)
        km_ref[:, sl] = (kn[:, sl] + kr).astype(km_ref.dtype)
    _store_values_t(vtm_ref, jnp.dot(ckvn, wv_ref[...], preferred_element_type=F32), ts)

    c = -_softplus(-(misc + bf_ref[...]))
    row = lax.broadcasted_iota(jnp.int32, (ts, LANES), 0)
    d = 1
    while d < ts:
        c = c + jnp.where(row >= d, pltpu.roll(c, d, 0), 0.0)
        d *= 2
    c = c + carry_scr[...]
    carry_scr[...] = c[ts - 1:ts, :]
    c = c * LOG2E

    terms = jnp.concatenate(_bf16_terms(c), axis=1).astype(BF16)
    nslot = FOX_HEADS * HEAD_SLOT
    q_bias = jnp.dot(terms, selq_ref[...], preferred_element_type=F32) + oneq_ref[...]
    k_bias = jnp.dot(terms, selk_ref[...], preferred_element_type=F32) + onek_ref[...]
    qf_ref[...] = (fox_ref[:, 0:nslot].astype(F32) + q_bias).astype(qf_ref.dtype)
    kf_ref[...] = (fox_ref[:, nslot:2 * nslot].astype(F32) + k_bias).astype(kf_ref.dtype)
    _store_values_t(vtf_ref, fox_ref[:, 2 * nslot:2 * nslot + FOX_WIDTH].astype(F32), ts)


def _prep(lat, fox, tabs, qn, kvn, wq, wk, wv, bf_row, bias_tabs, B, S, q_scale, ts=1024):
    NL, NF = lat.shape[-1], fox.shape[-1]
    full = lambda shape: pl.BlockSpec(shape, lambda b, t: (0,) * len(shape))
    tab = pl.BlockSpec((ts, LANES), lambda b, t: (t, 0))
    HS = MLA_HEADS * HEAD_SLOT
    slots = pl.BlockSpec((None, ts, HS), lambda b, t: (b, t, 0))
    vts = pl.BlockSpec((None, MLA_HEADS * V_ROWS, ts), lambda b, t: (b, 0, t))
    slot_shape = jax.ShapeDtypeStruct((B, S, HS), BF16)
    vt_shape = jax.ShapeDtypeStruct((B, MLA_HEADS * V_ROWS, S), BF16)
    return pl.pallas_call(
        functools.partial(_prep_kernel, ts=ts, q_scale=q_scale),
        grid=(B, S // ts),
        in_specs=[
            pl.BlockSpec((None, ts, NL), lambda b, t: (b, t, 0)),
            pl.BlockSpec((None, ts, NF), lambda b, t: (b, t, 0)),
            tab, tab, tab, tab,
            full((1, MLA_Q_LORA)), full((1, MLA_KV_LORA)),
            full((MLA_Q_LORA, HS)), full((MLA_KV_LORA, HS)), full((MLA_KV_LORA, MLA_HEADS * MLA_V)),
            full((1, LANES)),
            full((3 * LANES, HS)), full((3 * LANES, HS)), full((1, HS)), full((1, HS)),
        ],
        out_specs=[slots, slots, vts, slots, slots, vts],
        out_shape=[slot_shape, slot_shape, vt_shape, slot_shape, slot_shape, vt_shape],
        scratch_shapes=[pltpu.VMEM((1, LANES), F32)],
        compiler_params=_cparams(("parallel", "arbitrary")),
        name="mla_fox_prep",
    )(lat.reshape(B, S, NL), fox.reshape(B, S, NF), *tabs, qn.reshape(1, -1), kvn.reshape(1, -1), wq, wk, wv,
      bf_row, *bias_tabs)


def _attn_kernel(q_ref, k_ref, vt_ref, mask_ref, o_ref, m_scr, acc_scr, s_scr, *, tq, tk):
    qi = pl.program_id(2)
    nsub = tq // Q_SUB
    ndiag = tq // tk
    m_scr[...] = jnp.full(m_scr.shape, NEG_INF, F32)
    acc_scr[...] = jnp.zeros(acc_scr.shape, F32)

    def diag_kinds(d):
        kinds = []
        for c in range(nsub):
            if (c + 1) * Q_SUB <= d * tk:
                kinds.append(None)
            elif (d + 1) * tk <= c * Q_SUB:
                kinds.append((tk, None))
            else:
                kinds.append((min(tk, (c + 1) * Q_SUB - d * tk), d * tk))
        return kinds

    def units(k0, kinds, slot0):
        live = [(h, c) for c in range(nsub) for h in range(2) if kinds[c] is not None]
        return [(k0, kinds[c][0], kinds[c][1], h, c, slot0 + idx) for idx, (h, c) in enumerate(live)]

    def logits_one(k0, klen, h, c, slot):
        kh = k_ref[pl.ds(k0, klen), h * HEAD_SLOT:(h + 1) * HEAD_SLOT]
        qh = q_ref[c * Q_SUB:(c + 1) * Q_SUB, h * HEAD_SLOT:(h + 1) * HEAD_SLOT]
        s_scr[slot, 0:klen] = lax.dot_general(kh, qh, (((1,), (1,)), ((), ())), preferred_element_type=F32)

    def softmax_one(k0, klen, mask_off, h, c, slot):
        st = s_scr[slot, 0:klen]
        cs = slice(c * Q_SUB, (c + 1) * Q_SUB)
        if mask_off is not None:
            st = st + mask_ref[(c * Q_SUB - mask_off) // Q_SUB, 0:klen, :]
        m_prev = m_scr[h, :, cs]
        m_new = jnp.maximum(m_prev, jnp.max(st, axis=0, keepdims=True))
        alpha = jnp.exp2(m_prev - m_new)
        pt = jnp.exp2(st - m_new).astype(BF16)
        m_scr[h, :, cs] = m_new
        vth = vt_ref[h * V_ROWS:(h + 1) * V_ROWS, pl.ds(k0, klen)]
        acc_scr[h, :, cs] = alpha * acc_scr[h, :, cs] + jnp.dot(vth, pt, preferred_element_type=F32)

    def run(todo, ahead):
        for j in range(len(todo)):
            if j < len(ahead):
                k0, klen, _, h, c, slot = ahead[j]
                logits_one(k0, klen, h, c, slot)
            softmax_one(*todo[j])

    n = qi * ndiag
    per = 2 * nsub
    all_visible = [(tk, None)] * nsub
    for k0, klen, _, h, c, slot in units(0, all_visible, 0)[:SKEW]:
        logits_one(k0, klen, h, c, slot)

    def body(i, carry):
        chunks = [units(pl.multiple_of((LOOP_CHUNKS * i + t) * tk, tk), all_visible, (t % 2) * per)
                  for t in range(LOOP_CHUNKS + 1)]
        flat = sum(chunks, [])
        run(flat[:LOOP_CHUNKS * per], flat[SKEW:SKEW + LOOP_CHUNKS * per])
        return carry

    lax.fori_loop(0, n // LOOP_CHUNKS, body, 0)
    flat = []
    for d in range(ndiag):
        flat += units(pl.multiple_of(qi * tq + d * tk, tk), diag_kinds(d), (d % 2) * per)
    run(flat, flat[SKEW:])

    outs = []
    for h in range(2):
        a = acc_scr[h]
        outs.append(a[0:MLA_V, :] / a[MLA_V:MLA_V + 1, :])
    o_ref[...] = jnp.concatenate(outs, axis=0).T.astype(o_ref.dtype)


def _causal_bias(unit, tk):
    key = jnp.arange(tk)[None, :, None] // unit
    qry = (jnp.arange(Q_SUB)[None, None, :] + Q_SUB * jnp.arange(tk // Q_SUB)[:, None, None]) // unit
    return jnp.where(qry >= key, 0.0, NEG_INF).astype(F32)


def _attention(q, k, vt, B, S, *, unit, name, tq=2048, tk=512):
    npair = MLA_HEADS // 2
    nstream = 2 * (tq // Q_SUB)
    assert (tq // tk) % LOOP_CHUNKS == 0 and LOOP_CHUNKS % 2 == 0
    return pl.pallas_call(
        functools.partial(_attn_kernel, tq=tq, tk=tk),
        grid=(B, npair, S // tq),
        in_specs=[
            pl.BlockSpec((None, tq, 2 * HEAD_SLOT), lambda b, p, i: (b, i, p)),
            pl.BlockSpec((None, S, 2 * HEAD_SLOT), lambda b, p, i: (b, 0, p)),
            pl.BlockSpec((None, 2 * V_ROWS, S), lambda b, p, i: (b, p, 0)),
            pl.BlockSpec((tk // Q_SUB, tk, Q_SUB), lambda b, p, i: (0, 0, 0)),
        ],
        out_specs=pl.BlockSpec((None, tq, 2 * MLA_V), lambda b, p, i: (b, i, p)),
        out_shape=jax.ShapeDtypeStruct((B, S, npair * 2 * MLA_V), BF16),
        scratch_shapes=[pltpu.VMEM((2, 1, tq), F32), pltpu.VMEM((2, V_ROWS, tq), F32),
                        pltpu.VMEM((2 * nstream, tk, Q_SUB), F32)],
        compiler_params=_cparams(("parallel", "parallel", "arbitrary")),
        name=name,
    )(q, k, vt, _causal_bias(unit, tk))


def _merge_kernel(ya_ref, ob_ref, oc_ref, g_ref, x_ref, wa_ref, wb_ref, wc_ref, wo_ref, o_ref):
    D = x_ref.shape[-1]
    ya = jnp.dot(ya_ref[...], wa_ref[...], preferred_element_type=F32)
    yb = jnp.dot(ob_ref[...], wb_ref[...], preferred_element_type=F32)
    yc = jnp.dot(oc_ref[...], wc_ref[...], preferred_element_type=F32)
    merged = g_ref[:, 0:D] * ya + g_ref[:, D:2 * D] * yb + g_ref[:, 2 * D:3 * D] * yc
    o_ref[...] = x_ref[...] + jnp.dot(merged.astype(BF16), wo_ref[...], preferred_element_type=F32)


def _merge(ya, ob, oc, g, x, wa, wb, wc, wo, tm=512):
    T, D = x.shape
    W = ya.shape[-1]
    rows = lambda n: pl.BlockSpec((tm, n), lambda i: (i, 0))
    return pl.pallas_call(
        _merge_kernel,
        grid=(T // tm,),
        in_specs=[rows(W), rows(W), rows(W), rows(3 * D), rows(D),
                  _resident((W, D)), _resident((W, D)), _resident((W, D)), _resident((D, D))],
        out_specs=rows(D),
        out_shape=jax.ShapeDtypeStruct((T, D), F32),
        compiler_params=_cparams(("parallel",)),
        name="merge",
    )(ya, ob, oc, g, x, wa, wb, wc, wo)


FFN_TF = 1408


def _ffn_ple_kernel(x_ref, p_ref, fg_ref, wgu_ref, wd_ref, pn_ref, wpg_ref, wple_ref, fn_ref, o_ref, *, final):
    d_ff = wd_ref.shape[0]
    x = x_ref[...]
    h = _rms(x, fg_ref[...]).astype(BF16)
    for c in range(d_ff // FFN_TF):
        gate = jnp.dot(h, wgu_ref[:, c * FFN_TF:(c + 1) * FFN_TF], preferred_element_type=F32)
        up = jnp.dot(h, wgu_ref[:, d_ff + c * FFN_TF:d_ff + (c + 1) * FFN_TF], preferred_element_type=F32)
        act = (gate * _sigmoid(gate) * up).astype(BF16)
        x = x + jnp.dot(act, wd_ref[c * FFN_TF:(c + 1) * FFN_TF, :], preferred_element_type=F32)
    hn = _rms(x, pn_ref[...]).astype(BF16)
    pg = _sigmoid(jnp.dot(hn, wpg_ref[...], preferred_element_type=F32))
    pe = jnp.dot(p_ref[...].astype(BF16), wple_ref[...], preferred_element_type=F32)
    x = x + pg * pe
    if final:
        x = _rms(x, fn_ref[...])
    o_ref[...] = x


def _ffn_ple(x, p, layer, fg, w_gate_up, w_down, pn, wpg, wple, fn, final, tm=512):
    T, D = x.shape
    P = p.shape[-1]
    d_ff = w_down.shape[0]
    assert d_ff % FFN_TF == 0
    return pl.pallas_call(
        functools.partial(_ffn_ple_kernel, final=final),
        grid=(T // tm,),
        in_specs=[pl.BlockSpec((tm, D), lambda i: (i, 0)), pl.BlockSpec((None, tm, P), lambda i: (layer, i, 0)),
                  _resident((1, D)), _resident((D, 2 * d_ff)), _resident((d_ff, D)),
                  _resident((1, D)), _resident((D, D)), _resident((P, D)), _resident((1, D))],
        out_specs=pl.BlockSpec((tm, D), lambda i: (i, 0)),
        out_shape=jax.ShapeDtypeStruct((T, D), F32),
        compiler_params=_cparams(("parallel",)),
        name="ffn_ple_final" if final else "ffn_ple",
    )(x, p, fg.reshape(1, D), w_gate_up, w_down, pn.reshape(1, D), wpg, wple, fn.reshape(1, D))


def _block_diag(w):
    L, H, I, J = w.shape
    eye = jnp.eye(H, dtype=w.dtype)
    return (eye[None, :, None, :, None] * w[:, :, :, None, :]).reshape(L, H * I, H * J)


def _head_slots(w, width):
    lead = w.shape[:-1]
    w = w.reshape(*lead, MLA_HEADS, width)
    pad = [(0, 0)] * (w.ndim - 1) + [(0, HEAD_SLOT - width)]
    return jnp.pad(w, pad).reshape(*lead, MLA_HEADS * HEAD_SLOT)


def _rope_tables(S):
    pos = jnp.arange(S, dtype=F32)
    inv_freq = ROPE_BASE ** (-jnp.arange(0, MLA_ROPE, 2, dtype=F32) / MLA_ROPE)
    ang = pos[:, None] * inv_freq[None, :]
    cos, sin = jnp.cos(ang), jnp.sin(ang)
    half = MLA_ROPE // 2
    z = lambda n: jnp.zeros((S, n), F32)
    tail = LANES - ROPE_LANE0 - MLA_ROPE
    cq_t = jnp.concatenate([jnp.ones((S, ROPE_LANE0), F32), cos, cos, z(tail)], axis=1)
    ck_t = jnp.concatenate([z(ROPE_LANE0), cos, cos, z(tail)], axis=1)
    s1 = jnp.concatenate([z(ROPE_LANE0), -sin, z(half), z(tail)], axis=1)
    s2 = jnp.concatenate([z(ROPE_LANE0), z(half), sin, z(tail)], axis=1)
    return cq_t, ck_t, s1, s2


def kernel(x, p, mix_norm, w_in, gate_b, conv_w, conv_b, lru_wa, lru_ba, lru_wx, lru_bx, lru_lambda,
           mla_q_norm, mla_wuq, mla_kv_norm, mla_wukv, fox_bf, w_br_a, w_br_b, w_br_c, w_o,
           ffn_norm, w_gate_up, w_down, ple_norm, w_ple_gate, w_ple, final_norm):
    B, S, D = x.shape
    depth = w_in.shape[0]
    T = B * S
    tabs = _rope_tables(S)
    bias_tabs = _fox_bias_tables()
    mla_scale = (MLA_NOPE + MLA_ROPE) ** -0.5 * LOG2E
    fox_scale = FOX_HEAD_DIM ** -0.5 * LOG2E

    o_u = 0
    o_cq = 2 * LRU_WIDTH
    o_ckv = o_cq + MLA_Q_LORA
    o_kr = o_ckv + MLA_KV_LORA
    o_fq = o_kr + MLA_ROPE
    o_fl = o_fq + 3 * FOX_WIDTH
    o_g = o_fl + FOX_HEADS

    wb = w_in.astype(BF16)
    zcols = lambda n: jnp.zeros((depth, D, n), BF16)
    lat_w = MLA_Q_LORA + MLA_KV_LORA + LANES
    lat_pad = -lat_w % IN_TN
    nslot = FOX_HEADS * HEAD_SLOT
    w_all = jnp.concatenate([
        wb[..., o_u:o_cq],
        wb[..., o_cq:o_kr], zcols(ROPE_LANE0), wb[..., o_kr:o_fq], wb[..., o_fl:o_g],
        zcols(LANES - ROPE_LANE0 - MLA_ROPE - FOX_HEADS + lat_pad),
        _head_slots(wb[..., o_fq:o_fq + FOX_WIDTH], FOX_HEAD_DIM),
        _head_slots(wb[..., o_fq + FOX_WIDTH:o_fq + 2 * FOX_WIDTH], FOX_HEAD_DIM),
        wb[..., o_fq + 2 * FOX_WIDTH:o_fl],
        wb[..., o_g:]], axis=-1)
    widths = (2 * LRU_WIDTH, lat_w + lat_pad, 2 * nslot + FOX_WIDTH, 3 * D)
    n_before_fox, n_before_gate = widths[0] + widths[1], widths[0] + widths[1] + widths[2]
    cs = jnp.ones((1, sum(widths)), F32).at[0, n_before_fox:n_before_fox + nslot].set(fox_scale)
    cb = jnp.concatenate([jnp.zeros((depth, n_before_gate), F32), gate_b], axis=1)[:, None, :]
    wa_bd, wx_bd = _block_diag(lru_wa).astype(BF16), _block_diag(lru_wx).astype(BF16)
    wq = _head_slots(mla_wuq.astype(BF16), MLA_NOPE + MLA_ROPE)
    wukv = mla_wukv.astype(BF16).reshape(depth, MLA_KV_LORA, MLA_HEADS, MLA_NOPE + MLA_V)
    wk = _head_slots(wukv[..., :MLA_NOPE].reshape(depth, MLA_KV_LORA, MLA_HEADS * MLA_NOPE), MLA_NOPE)
    wv = wukv[..., MLA_NOPE:].reshape(depth, MLA_KV_LORA, MLA_HEADS * MLA_V)
    bf_rows = jnp.zeros((depth, 1, LANES), F32).at[:, 0, FLOGIT_LANE0:FLOGIT_LANE0 + FOX_HEADS].set(fox_bf)
    wbr_a, wbr_b, wbr_c, wo = (w.astype(BF16) for w in (w_br_a, w_br_b, w_br_c, w_o))
    wgu, wdn, wpg, wpl = (w.astype(BF16) for w in (w_gate_up, w_down, w_ple_gate, w_ple))

    xf = x.reshape(T, D)
    for i in range(depth):
        u, lat, fox, g = _in_proj(xf, mix_norm[i], w_all[i], cs, cb[i], widths)
        ya = _lru_branch(u, conv_w[i], conv_b[i], wa_bd[i], lru_ba[i], wx_bd[i], lru_bx[i], lru_lambda[i], B, S)
        q_m, k_m, vt_m, q_f, k_f, vt_f = _prep(lat, fox, tabs, mla_q_norm[i], mla_kv_norm[i], wq[i], wk[i], wv[i],
                                               bf_rows[i], bias_tabs, B, S, mla_scale)
        ob = _attention(q_m, k_m, vt_m, B, S, unit=MLA_CHUNK, name="mla_attention")
        oc = _attention(q_f, k_f, vt_f, B, S, unit=1, name="fox_attention")
        xf = _merge(ya.reshape(T, -1), ob.reshape(T, -1), oc.reshape(T, -1), g, xf,
                    wbr_a[i], wbr_b[i], wbr_c[i], wo[i])
        xf = _ffn_ple(xf, p.reshape(depth, T, -1), i, ffn_norm[i], wgu[i], wdn[i], ple_norm[i], wpg[i], wpl[i],
                      final_norm, final=(i == depth - 1))
    return xf.reshape(B, S, D)
```

```python
import functools
import math

import jax
import jax.numpy as jnp
from jax import lax
from jax.experimental import pallas as pl
from jax.experimental.pallas import tpu as pltpu

F32 = jnp.float32
BF16 = jnp.bfloat16

EPS = 1e-6
NEG_INF = -1e30
LOG2E = math.log2(math.e)

LRU_WIDTH = 512
LRU_HEADS = 8
CONV_WIDTH = 4
LRU_C = 8.0

MLA_HEADS = 8
MLA_Q_LORA = 384
MLA_KV_LORA = 256
MLA_NOPE = 64
MLA_ROPE = 32
MLA_V = 64
ROPE_BASE = 10000.0
MLA_CHUNK = 64

FOX_HEADS = 8
FOX_HEAD_DIM = 64
FOX_WIDTH = FOX_HEADS * FOX_HEAD_DIM

LANES = 128
SUBLANES = 8
HEAD_SLOT = LANES
ROPE_LANE0 = MLA_NOPE
FLOGIT_LANE0 = MLA_NOPE + MLA_ROPE
V_ROWS = 80
Q_SUB = 256
SKEW = 4
LOOP_CHUNKS = 4

VMEM_LIMIT = 56 * 1024 * 1024


def _cparams(sem):
    return pltpu.CompilerParams(dimension_semantics=sem, vmem_limit_bytes=VMEM_LIMIT)


def _rms(xf, g):
    ms = jnp.mean(xf * xf, axis=-1, keepdims=True)
    return xf * lax.rsqrt(ms + EPS) * g


def _sigmoid(z):
    return 0.5 * jnp.tanh(0.5 * z) + 0.5


def _softplus(z):
    return jnp.maximum(z, 0.0) + jnp.log1p(jnp.exp(-jnp.abs(z)))


IN_TN = 256


def _resident(shape):
    return pl.BlockSpec(shape, lambda *_: (0,) * len(shape), pipeline_mode=pl.Buffered(1))


def _in_proj_lru_kernel(x_ref, g_ref, w_ref, cs_ref, cb_ref, cw_ref, cvb_ref, wa_ref, ba_ref, wx_ref, bx_ref, lam_ref,
                        ya_ref, lat_ref, fox_ref, gate_ref, ext_scr, hc_scr, *, widths, tm):
    W = LRU_WIDTH
    pad = SUBLANES

    @pl.when(pl.program_id(1) == 0)
    def _():
        ext_scr[0:pad, :] = jnp.zeros((pad, W), F32)
        hc_scr[...] = jnp.zeros((1, W), F32)

    h = _rms(x_ref[...], g_ref[...]).astype(BF16)

    def tile(j):
        sl = slice(j * IN_TN, (j + 1) * IN_TN)
        return jnp.dot(h, w_ref[:, sl], preferred_element_type=F32), sl

    ntu = widths[0] // IN_TN
    uu = jnp.concatenate([tile(j)[0] for j in range(ntu)], axis=1)
    u, ug = uu[:, 0:W], uu[:, W:2 * W]
    ext_scr[pad:pad + tm, :] = u
    xc = cvb_ref[...] + ext_scr[pl.ds(pad - 3, tm), :] * cw_ref[0:1, :]
    xc = xc + ext_scr[pl.ds(pad - 2, tm), :] * cw_ref[1:2, :]
    xc = xc + ext_scr[pl.ds(pad - 1, tm), :] * cw_ref[2:3, :]
    xc = xc + u * cw_ref[3:4, :]
    ext_scr[0:pad, :] = u[tm - pad:tm, :]
    xb = xc.astype(BF16)
    r = _sigmoid(jnp.dot(xb, wa_ref[...], preferred_element_type=F32) + ba_ref[...])
    ig = _sigmoid(jnp.dot(xb, wx_ref[...], preferred_element_type=F32) + bx_ref[...])
    log_a = (-LRU_C) * r * _softplus(-lam_ref[...])
    a = jnp.exp(log_a)
    b = jnp.sqrt(jnp.tanh(-log_a) * (a * a + 1.0)) * (ig * xc)

    row = lax.broadcasted_iota(jnp.int32, (SUBLANES, W), 0)
    h_prev = hc_scr[...]
    hs = []
    for i in range(tm // SUBLANES):
        aa = a[i * SUBLANES:(i + 1) * SUBLANES, :]
        bb = b[i * SUBLANES:(i + 1) * SUBLANES, :]
        d = 1
        while d < SUBLANES:
            keep = row >= d
            a_sh = jnp.where(keep, pltpu.roll(aa, d, 0), 1.0)
            b_sh = jnp.where(keep, pltpu.roll(bb, d, 0), 0.0)
            bb = aa * b_sh + bb
            aa = aa * a_sh
            d *= 2
        hb = bb + aa * h_prev
        hs.append(hb)
        h_prev = hb[SUBLANES - 1:SUBLANES, :]
    hc_scr[...] = h_prev
    ya_ref[...] = (jnp.concatenate(hs, axis=0) * jax.nn.gelu(ug)).astype(ya_ref.dtype)

    col = widths[0]
    for slab, (o_ref, width) in enumerate(zip((lat_ref, fox_ref, gate_ref), widths[1:])):
        for j in range(width // IN_TN):
            acc, sl = tile(col // IN_TN + j)
            if slab == 1:
                acc = acc * cs_ref[:, sl]
            if slab == 2:
                acc = _sigmoid(acc + cb_ref[:, sl])
            o_ref[:, j * IN_TN:(j + 1) * IN_TN] = acc.astype(o_ref.dtype)
        col += width


def _in_proj_lru(x, g, w_all, cs, cb, widths, conv_w, conv_b, wa_bd, ba, wx_bd, bx, lam, B, S, tm=512):
    D = x.shape[-1]
    N = sum(widths)
    W = LRU_WIDTH
    rowv = lambda v: v.reshape(1, W)
    out = lambda w: pl.BlockSpec((None, tm, w), lambda b, t: (b, t, 0))
    return pl.pallas_call(
        functools.partial(_in_proj_lru_kernel, widths=widths, tm=tm),
        grid=(B, S // tm),
        in_specs=[
            pl.BlockSpec((None, tm, D), lambda b, t: (b, t, 0)),
            _resident((1, D)), _resident((D, N)), _resident((1, N)), _resident((1, N)),
            _resident((CONV_WIDTH, W)), _resident((1, W)),
            _resident((W, W)), _resident((1, W)), _resident((W, W)), _resident((1, W)), _resident((1, W)),
        ],
        out_specs=[out(W), out(widths[1]), out(widths[2]), out(widths[3])],
        out_shape=[jax.ShapeDtypeStruct((B, S, W), BF16), jax.ShapeDtypeStruct((B, S, widths[1]), F32),
                   jax.ShapeDtypeStruct((B, S, widths[2]), BF16), jax.ShapeDtypeStruct((B, S, widths[3]), BF16)],
        scratch_shapes=[pltpu.VMEM((tm + SUBLANES, W), F32), pltpu.VMEM((1, W), F32)],
        compiler_params=_cparams(("parallel", "arbitrary")),
        name="in_proj_lru",
    )(x.reshape(B, S, D), g.reshape(1, D), w_all, cs, cb, conv_w, rowv(conv_b), wa_bd, rowv(ba), wx_bd, rowv(bx),
      rowv(lam))


def _rope_slot(x, c, s1, s2):
    return x * c + pltpu.roll(x, LANES - MLA_ROPE // 2, 1) * s1 + pltpu.roll(x, MLA_ROPE // 2, 1) * s2


def _bf16_terms(c):
    hi = c.astype(BF16).astype(F32)
    r = c - hi
    lo = r.astype(BF16).astype(F32)
    return hi, lo, (r - lo).astype(BF16).astype(F32)


def _fox_bias_tables():
    rows = jnp.arange(3 * LANES)
    term, head = rows // LANES, rows % LANES - FLOGIT_LANE0
    valid = (head >= 0) & (head < FOX_HEADS)
    cols = jnp.arange(FOX_HEADS * HEAD_SLOT)
    col_head, col_lane = cols // HEAD_SLOT, cols % HEAD_SLOT - FOX_HEAD_DIM
    hit = valid[:, None] & (head[:, None] == col_head[None, :])
    sel_q = jnp.where(hit & (col_lane[None, :] == term[:, None]), 1.0, 0.0).astype(BF16)
    sel_k = jnp.where(hit & (col_lane[None, :] == term[:, None] + 3), -1.0, 0.0).astype(BF16)
    one_q = jnp.where((col_lane >= 3) & (col_lane < 6), 1.0, 0.0).astype(F32)[None, :]
    one_k = jnp.where((col_lane >= 0) & (col_lane < 3), 1.0, 0.0).astype(F32)[None, :]
    return sel_q, sel_k, one_q, one_k


def _store_values_t(vt_ref, v, ts):
    vt = v.T
    row = lax.broadcasted_iota(jnp.int32, (V_ROWS - MLA_V, ts), 0)
    ones_tile = jnp.where(row == 0, 1.0, 0.0).astype(vt_ref.dtype)
    for h in range(MLA_HEADS):
        vt_ref[h * V_ROWS:h * V_ROWS + MLA_V, :] = vt[h * MLA_V:(h + 1) * MLA_V, :].astype(vt_ref.dtype)
        vt_ref[h * V_ROWS + MLA_V:(h + 1) * V_ROWS, :] = ones_tile


def _prep_kernel(lat_ref, fox_ref, cq_t_ref, ck_t_ref, s1_ref, s2_ref, qn_ref, kvn_ref, wq_ref, wk_ref, wv_ref,
                 bf_ref, selq_ref, selk_ref, oneq_ref, onek_ref, qm_ref, km_ref, vtm_ref, qf_ref, kf_ref, vtf_ref,
                 carry_scr, *, ts, q_scale):
    @pl.when(pl.program_id(1) == 0)
    def _():
        carry_scr[...] = jnp.zeros((1, LANES), F32)

    cq_t, ck_t, s1, s2 = cq_t_ref[...], ck_t_ref[...], s1_ref[...], s2_ref[...]

    cqn = _rms(lat_ref[:, 0:MLA_Q_LORA], qn_ref[...]).astype(BF16)
    q = jnp.dot(cqn, wq_ref[...], preferred_element_type=F32)
    for h in range(MLA_HEADS):
        sl = slice(h * HEAD_SLOT, (h + 1) * HEAD_SLOT)
        qm_ref[:, sl] = (_rope_slot(q[:, sl], cq_t, s1, s2) * q_scale).astype(qm_ref.dtype)

    ckvn = _rms(lat_ref[:, MLA_Q_LORA:MLA_Q_LORA + MLA_KV_LORA], kvn_ref[...]).astype(BF16)
    kn = jnp.dot(ckvn, wk_ref[...], preferred_element_type=F32)
    misc = lat_ref[:, MLA_Q_LORA + MLA_KV_LORA:MLA_Q_LORA + MLA_KV_LORA + LANES]
    kr = _rope_slot(misc, ck_t, s1, s2)
    for h in range(MLA_HEADS):
        sl = slice(h * HEAD_SLOT, (h + 1) * HEAD_SLOT)
        km_ref[:, sl] = (kn[:, sl] + kr).astype(km_ref.dtype)
    _store_values_t(vtm_ref, jnp.dot(ckvn, wv_ref[...], preferred_element_type=F32), ts)

    c = -_softplus(-(misc + bf_ref[...]))
    row = lax.broadcasted_iota(jnp.int32, (ts, LANES), 0)
    d = 1
    while d < ts:
        c = c + jnp.where(row >= d, pltpu.roll(c, d, 0), 0.0)
        d *= 2
    c = c + carry_scr[...]
    carry_scr[...] = c[ts - 1:ts, :]
    c = c * LOG2E

    terms = jnp.concatenate(_bf16_terms(c), axis=1).astype(BF16)
    nslot = FOX_HEADS * HEAD_SLOT
    q_bias = jnp.dot(terms, selq_ref[...], preferred_element_type=F32) + oneq_ref[...]
    k_bias = jnp.dot(terms, selk_ref[...], preferred_element_type=F32) + onek_ref[...]
    qf_ref[...] = (fox_ref[:, 0:nslot].astype(F32) + q_bias).astype(qf_ref.dtype)
    kf_ref[...] = (fox_ref[:, nslot:2 * nslot].astype(F32) + k_bias).astype(kf_ref.dtype)
    _store_values_t(vtf_ref, fox_ref[:, 2 * nslot:2 * nslot + FOX_WIDTH].astype(F32), ts)


def _prep(lat, fox, tabs, qn, kvn, wq, wk, wv, bf_row, bias_tabs, B, S, q_scale, ts=1024):
    NL, NF = lat.shape[-1], fox.shape[-1]
    full = lambda shape: pl.BlockSpec(shape, lambda b, t: (0,) * len(shape))
    tab = pl.BlockSpec((ts, LANES), lambda b, t: (t, 0))
    HS = MLA_HEADS * HEAD_SLOT
    slots = pl.BlockSpec((None, ts, HS), lambda b, t: (b, t, 0))
    vts = pl.BlockSpec((None, MLA_HEADS * V_ROWS, ts), lambda b, t: (b, 0, t))
    slot_shape = jax.ShapeDtypeStruct((B, S, HS), BF16)
    vt_shape = jax.ShapeDtypeStruct((B, MLA_HEADS * V_ROWS, S), BF16)
    return pl.pallas_call(
        functools.partial(_prep_kernel, ts=ts, q_scale=q_scale),
        grid=(B, S // ts),
        in_specs=[
            pl.BlockSpec((None, ts, NL), lambda b, t: (b, t, 0)),
            pl.BlockSpec((None, ts, NF), lambda b, t: (b, t, 0)),
            tab, tab, tab, tab,
            full((1, MLA_Q_LORA)), full((1, MLA_KV_LORA)),
            full((MLA_Q_LORA, HS)), full((MLA_KV_LORA, HS)), full((MLA_KV_LORA, MLA_HEADS * MLA_V)),
            full((1, LANES)),
            full((3 * LANES, HS)), full((3 * LANES, HS)), full((1, HS)), full((1, HS)),
        ],
        out_specs=[slots, slots, vts, slots, slots, vts],
        out_shape=[slot_shape, slot_shape, vt_shape, slot_shape, slot_shape, vt_shape],
        scratch_shapes=[pltpu.VMEM((1, LANES), F32)],
        compiler_params=_cparams(("parallel", "arbitrary")),
        name="mla_fox_prep",
    )(lat.reshape(B, S, NL), fox.reshape(B, S, NF), *tabs, qn.reshape(1, -1), kvn.reshape(1, -1), wq, wk, wv,
      bf_row, *bias_tabs)


def _attn_kernel(q_ref, k_ref, vt_ref, mask_ref, o_ref, m_scr, acc_scr, s_scr, *, tq, tk):
    qi = pl.program_id(2)
    nsub = tq // Q_SUB
    ndiag = tq // tk
    m_scr[...] = jnp.full(m_scr.shape, NEG_INF, F32)
    acc_scr[...] = jnp.zeros(acc_scr.shape, F32)

    def diag_kinds(d):
        kinds = []
        for c in range(nsub):
            if (c + 1) * Q_SUB <= d * tk:
                kinds.append(None)
            elif (d + 1) * tk <= c * Q_SUB:
                kinds.append((tk, None))
            else:
                kinds.append((min(tk, (c + 1) * Q_SUB - d * tk), d * tk))
        return kinds

    def units(k0, kinds, slot0):
        live = [(h, c) for c in range(nsub) for h in range(2) if kinds[c] is not None]
        return [(k0, kinds[c][0], kinds[c][1], h, c, slot0 + idx) for idx, (h, c) in enumerate(live)]

    def logits_one(k0, klen, h, c, slot):
        kh = k_ref[pl.ds(k0, klen), h * HEAD_SLOT:(h + 1) * HEAD_SLOT]
        qh = q_ref[c * Q_SUB:(c + 1) * Q_SUB, h * HEAD_SLOT:(h + 1) * HEAD_SLOT]
        s_scr[slot, 0:klen] = lax.dot_general(kh, qh, (((1,), (1,)), ((), ())), preferred_element_type=F32)

    def softmax_one(k0, klen, mask_off, h, c, slot):
        st = s_scr[slot, 0:klen]
        cs = slice(c * Q_SUB, (c + 1) * Q_SUB)
        if mask_off is not None:
            st = st + mask_ref[(c * Q_SUB - mask_off) // Q_SUB, 0:klen, :]
        m_prev = m_scr[h, :, cs]
        m_new = jnp.maximum(m_prev, jnp.max(st, axis=0, keepdims=True))
        alpha = jnp.exp2(m_prev - m_new)
        pt = jnp.exp2(st - m_new).astype(BF16)
        m_scr[h, :, cs] = m_new
        vth = vt_ref[h * V_ROWS:(h + 1) * V_ROWS, pl.ds(k0, klen)]
        acc_scr[h, :, cs] = alpha * acc_scr[h, :, cs] + jnp.dot(vth, pt, preferred_element_type=F32)

    def run(todo, ahead):
        for j in range(len(todo)):
            if j < len(ahead):
                k0, klen, _, h, c, slot = ahead[j]
                logits_one(k0, klen, h, c, slot)
            softmax_one(*todo[j])

    n = qi * ndiag
    per = 2 * nsub
    all_visible = [(tk, None)] * nsub
    for k0, klen, _, h, c, slot in units(0, all_visible, 0)[:SKEW]:
        logits_one(k0, klen, h, c, slot)

    def body(i, carry):
        chunks = [units(pl.multiple_of((LOOP_CHUNKS * i + t) * tk, tk), all_visible, (t % 2) * per)
                  for t in range(LOOP_CHUNKS + 1)]
        flat = sum(chunks, [])
        run(flat[:LOOP_CHUNKS * per], flat[SKEW:SKEW + LOOP_CHUNKS * per])
        return carry

    lax.fori_loop(0, n // LOOP_CHUNKS, body, 0)
    flat = []
    for d in range(ndiag):
        flat += units(pl.multiple_of(qi * tq + d * tk, tk), diag_kinds(d), (d % 2) * per)
    run(flat, flat[SKEW:])

    outs = []
    for h in range(2):
        a = acc_scr[h]
        outs.append(a[0:MLA_V, :] / a[MLA_V:MLA_V + 1, :])
    o_ref[...] = jnp.concatenate(outs, axis=0).T.astype(o_ref.dtype)


def _causal_bias(unit, tk):
    key = jnp.arange(tk)[None, :, None] // unit
    qry = (jnp.arange(Q_SUB)[None, None, :] + Q_SUB * jnp.arange(tk // Q_SUB)[:, None, None]) // unit
    return jnp.where(qry >= key, 0.0, NEG_INF).astype(F32)


def _attention(q, k, vt, B, S, *, unit, name, tq=2048, tk=512):
    npair = MLA_HEADS // 2
    nstream = 2 * (tq // Q_SUB)
    assert (tq // tk) % LOOP_CHUNKS == 0 and LOOP_CHUNKS % 2 == 0
    return pl.pallas_call(
        functools.partial(_attn_kernel, tq=tq, tk=tk),
        grid=(B, npair, S // tq),
        in_specs=[
            pl.BlockSpec((None, tq, 2 * HEAD_SLOT), lambda b, p, i: (b, i, p)),
            pl.BlockSpec((None, S, 2 * HEAD_SLOT), lambda b, p, i: (b, 0, p)),
            pl.BlockSpec((None, 2 * V_ROWS, S), lambda b, p, i: (b, p, 0)),
            pl.BlockSpec((tk // Q_SUB, tk, Q_SUB), lambda b, p, i: (0, 0, 0)),
        ],
        out_specs=pl.BlockSpec((None, tq, 2 * MLA_V), lambda b, p, i: (b, i, p)),
        out_shape=jax.ShapeDtypeStruct((B, S, npair * 2 * MLA_V), BF16),
        scratch_shapes=[pltpu.VMEM((2, 1, tq), F32), pltpu.VMEM((2, V_ROWS, tq), F32),
                        pltpu.VMEM((2 * nstream, tk, Q_SUB), F32)],
        compiler_params=_cparams(("parallel", "parallel", "arbitrary")),
        name=name,
    )(q, k, vt, _causal_bias(unit, tk))


def _merge_kernel(ya_ref, ob_ref, oc_ref, g_ref, x_ref, wa_ref, wb_ref, wc_ref, wo_ref, o_ref):
    D = x_ref.shape[-1]
    ya = jnp.dot(ya_ref[...], wa_ref[...], preferred_element_type=F32)
    yb = jnp.dot(ob_ref[...], wb_ref[...], preferred_element_type=F32)
    yc = jnp.dot(oc_ref[...], wc_ref[...], preferred_element_type=F32)
    merged = g_ref[:, 0:D] * ya + g_ref[:, D:2 * D] * yb + g_ref[:, 2 * D:3 * D] * yc
    o_ref[...] = x_ref[...] + jnp.dot(merged.astype(BF16), wo_ref[...], preferred_element_type=F32)


def _merge(ya, ob, oc, g, x, wa, wb, wc, wo, tm=512):
    T, D = x.shape
    W = ya.shape[-1]
    rows = lambda n: pl.BlockSpec((tm, n), lambda i: (i, 0))
    return pl.pallas_call(
        _merge_kernel,
        grid=(T // tm,),
        in_specs=[rows(W), rows(W), rows(W), rows(3 * D), rows(D),
                  _resident((W, D)), _resident((W, D)), _resident((W, D)), _resident((D, D))],
        out_specs=rows(D),
        out_shape=jax.ShapeDtypeStruct((T, D), F32),
        compiler_params=_cparams(("parallel",)),
        name="merge",
    )(ya, ob, oc, g, x, wa, wb, wc, wo)


FFN_TF = 1408


def _ffn_ple_kernel(x_ref, p_ref, fg_ref, wgu_ref, wd_ref, pn_ref, wpg_ref, wple_ref, fn_ref, o_ref, *, final):
    d_ff = wd_ref.shape[0]
    x = x_ref[...]
    h = _rms(x, fg_ref[...]).astype(BF16)
    for c in range(d_ff // FFN_TF):
        gate = jnp.dot(h, wgu_ref[:, c * FFN_TF:(c + 1) * FFN_TF], preferred_element_type=F32)
        up = jnp.dot(h, wgu_ref[:, d_ff + c * FFN_TF:d_ff + (c + 1) * FFN_TF], preferred_element_type=F32)
        act = (gate * _sigmoid(gate) * up).astype(BF16)
        x = x + jnp.dot(act, wd_ref[c * FFN_TF:(c + 1) * FFN_TF, :], preferred_element_type=F32)
    hn = _rms(x, pn_ref[...]).astype(BF16)
    pg = _sigmoid(jnp.dot(hn, wpg_ref[...], preferred_element_type=F32))
    pe = jnp.dot(p_ref[...].astype(BF16), wple_ref[...], preferred_element_type=F32)
    x = x + pg * pe
    if final:
        x = _rms(x, fn_ref[...])
    o_ref[...] = x


def _ffn_ple(x, p, layer, fg, w_gate_up, w_down, pn, wpg, wple, fn, final, tm=512):
    T, D = x.shape
    P = p.shape[-1]
    d_ff = w_down.shape[0]
    assert d_ff % FFN_TF == 0
    return pl.pallas_call(
        functools.partial(_ffn_ple_kernel, final=final),
        grid=(T // tm,),
        in_specs=[pl.BlockSpec((tm, D), lambda i: (i, 0)), pl.BlockSpec((None, tm, P), lambda i: (layer, i, 0)),
                  _resident((1, D)), _resident((D, 2 * d_ff)), _resident((d_ff, D)),
                  _resident((1, D)), _resident((D, D)), _resident((P, D)), _resident((1, D))],
        out_specs=pl.BlockSpec((tm, D), lambda i: (i, 0)),
        out_shape=jax.ShapeDtypeStruct((T, D), F32),
        compiler_params=_cparams(("parallel",)),
        name="ffn_ple_final" if final else "ffn_ple",
    )(x, p, fg.reshape(1, D), w_gate_up, w_down, pn.reshape(1, D), wpg, wple, fn.reshape(1, D))


def _block_diag(w):
    L, H, I, J = w.shape
    eye = jnp.eye(H, dtype=w.dtype)
    return (eye[None, :, None, :, None] * w[:, :, :, None, :]).reshape(L, H * I, H * J)


def _head_slots(w, width):
    lead = w.shape[:-1]
    w = w.reshape(*lead, MLA_HEADS, width)
    pad = [(0, 0)] * (w.ndim - 1) + [(0, HEAD_SLOT - width)]
    return jnp.pad(w, pad).reshape(*lead, MLA_HEADS * HEAD_SLOT)


def _rope_tables(S):
    pos = jnp.arange(S, dtype=F32)
    inv_freq = ROPE_BASE ** (-jnp.arange(0, MLA_ROPE, 2, dtype=F32) / MLA_ROPE)
    ang = pos[:, None] * inv_freq[None, :]
    cos, sin = jnp.cos(ang), jnp.sin(ang)
    half = MLA_ROPE // 2
    z = lambda n: jnp.zeros((S, n), F32)
    tail = LANES - ROPE_LANE0 - MLA_ROPE
    cq_t = jnp.concatenate([jnp.ones((S, ROPE_LANE0), F32), cos, cos, z(tail)], axis=1)
    ck_t = jnp.concatenate([z(ROPE_LANE0), cos, cos, z(tail)], axis=1)
    s1 = jnp.concatenate([z(ROPE_LANE0), -sin, z(half), z(tail)], axis=1)
    s2 = jnp.concatenate([z(ROPE_LANE0), z(half), sin, z(tail)], axis=1)
    return cq_t, ck_t, s1, s2


def kernel(x, p, mix_norm, w_in, gate_b, conv_w, conv_b, lru_wa, lru_ba, lru_wx, lru_bx, lru_lambda,
           mla_q_norm, mla_wuq, mla_kv_norm, mla_wukv, fox_bf, w_br_a, w_br_b, w_br_c, w_o,
           ffn_norm, w_gate_up, w_down, ple_norm, w_ple_gate, w_ple, final_norm):
    B, S, D = x.shape
    depth = w_in.shape[0]
    T = B * S
    tabs = _rope_tables(S)
    bias_tabs = _fox_bias_tables()
    mla_scale = (MLA_NOPE + MLA_ROPE) ** -0.5 * LOG2E
    fox_scale = FOX_HEAD_DIM ** -0.5 * LOG2E

    o_u = 0
    o_cq = 2 * LRU_WIDTH
    o_ckv = o_cq + MLA_Q_LORA
    o_kr = o_ckv + MLA_KV_LORA
    o_fq = o_kr + MLA_ROPE
    o_fl = o_fq + 3 * FOX_WIDTH
    o_g = o_fl + FOX_HEADS

    wb = w_in.astype(BF16)
    zcols = lambda n: jnp.zeros((depth, D, n), BF16)
    lat_w = MLA_Q_LORA + MLA_KV_LORA + LANES
    lat_pad = -lat_w % IN_TN
    nslot = FOX_HEADS * HEAD_SLOT
    w_all = jnp.concatenate([
        wb[..., o_u:o_cq],
        wb[..., o_cq:o_kr], zcols(ROPE_LANE0), wb[..., o_kr:o_fq], wb[..., o_fl:o_g],
        zcols(LANES - ROPE_LANE0 - MLA_ROPE - FOX_HEADS + lat_pad),
        _head_slots(wb[..., o_fq:o_fq + FOX_WIDTH], FOX_HEAD_DIM),
        _head_slots(wb[..., o_fq + FOX_WIDTH:o_fq + 2 * FOX_WIDTH], FOX_HEAD_DIM),
        wb[..., o_fq + 2 * FOX_WIDTH:o_fl],
        wb[..., o_g:]], axis=-1)
    widths = (2 * LRU_WIDTH, lat_w + lat_pad, 2 * nslot + FOX_WIDTH, 3 * D)
    n_before_fox, n_before_gate = widths[0] + widths[1], widths[0] + widths[1] + widths[2]
    cs = jnp.ones((1, sum(widths)), F32).at[0, n_before_fox:n_before_fox + nslot].set(fox_scale)
    cb = jnp.concatenate([jnp.zeros((depth, n_before_gate), F32), gate_b], axis=1)[:, None, :]
    wa_bd, wx_bd = _block_diag(lru_wa).astype(BF16), _block_diag(lru_wx).astype(BF16)
    wq = _head_slots(mla_wuq.astype(BF16), MLA_NOPE + MLA_ROPE)
    wukv = mla_wukv.astype(BF16).reshape(depth, MLA_KV_LORA, MLA_HEADS, MLA_NOPE + MLA_V)
    wk = _head_slots(wukv[..., :MLA_NOPE].reshape(depth, MLA_KV_LORA, MLA_HEADS * MLA_NOPE), MLA_NOPE)
    wv = wukv[..., MLA_NOPE:].reshape(depth, MLA_KV_LORA, MLA_HEADS * MLA_V)
    bf_rows = jnp.zeros((depth, 1, LANES), F32).at[:, 0, FLOGIT_LANE0:FLOGIT_LANE0 + FOX_HEADS].set(fox_bf)
    wbr_a, wbr_b, wbr_c, wo = (w.astype(BF16) for w in (w_br_a, w_br_b, w_br_c, w_o))
    wgu, wdn, wpg, wpl = (w.astype(BF16) for w in (w_gate_up, w_down, w_ple_gate, w_ple))

    xf = x.reshape(T, D)
    for i in range(depth):
        ya, lat, fox, g = _in_proj_lru(xf, mix_norm[i], w_all[i], cs, cb[i], widths, conv_w[i], conv_b[i], wa_bd[i],
                                       lru_ba[i], wx_bd[i], lru_bx[i], lru_lambda[i], B, S)
        q_m, k_m, vt_m, q_f, k_f, vt_f = _prep(lat, fox, tabs, mla_q_norm[i], mla_kv_norm[i], wq[i], wk[i], wv[i],
                                               bf_rows[i], bias_tabs, B, S, mla_scale)
        ob = _attention(q_m, k_m, vt_m, B, S, unit=MLA_CHUNK, name="mla_attention")
        oc = _attention(q_f, k_f, vt_f, B, S, unit=1, name="fox_attention")
        xf = _merge(ya.reshape(T, -1), ob.reshape(T, -1), oc.reshape(T, -1), g.reshape(T, -1), xf,
                    wbr_a[i], wbr_b[i], wbr_c[i], wo[i])
        xf = _ffn_ple(xf, p.reshape(depth, T, -1), i, ffn_norm[i], wgu[i], wdn[i], ple_norm[i], wpg[i], wpl[i],
                      final_norm, final=(i == depth - 1))
    return xf.reshape(B, S, D)
```

```python
import functools
import math

import jax
import jax.numpy as jnp
from jax import lax
from jax.experimental import pallas as pl
from jax.experimental.pallas import tpu as pltpu

F32 = jnp.float32
BF16 = jnp.bfloat16

EPS = 1e-6
NEG_INF = -1e30
LOG2E = math.log2(math.e)

LRU_WIDTH = 512
LRU_HEADS = 8
CONV_WIDTH = 4
LRU_C = 8.0

MLA_HEADS = 8
MLA_Q_LORA = 384
MLA_KV_LORA = 256
MLA_NOPE = 64
MLA_ROPE = 32
MLA_V = 64
ROPE_BASE = 10000.0
MLA_CHUNK = 64

FOX_HEADS = 8
FOX_HEAD_DIM = 64
FOX_WIDTH = FOX_HEADS * FOX_HEAD_DIM

LANES = 128
SUBLANES = 8
HEAD_SLOT = LANES
ROPE_LANE0 = MLA_NOPE
FLOGIT_LANE0 = MLA_NOPE + MLA_ROPE
V_ROWS = 80
Q_SUB = 256
SKEW = 4
LOOP_CHUNKS = 4

VMEM_LIMIT = 56 * 1024 * 1024


def _cparams(sem):
    return pltpu.CompilerParams(dimension_semantics=sem, vmem_limit_bytes=VMEM_LIMIT)


def _rms(xf, g):
    ms = jnp.mean(xf * xf, axis=-1, keepdims=True)
    return xf * lax.rsqrt(ms + EPS) * g


def _sigmoid(z):
    return 0.5 * jnp.tanh(0.5 * z) + 0.5


def _softplus(z):
    return jnp.maximum(z, 0.0) + jnp.log1p(jnp.exp(-jnp.abs(z)))


IN_TN = 256


def _resident(shape, layer=None):
    if layer is None:
        return pl.BlockSpec(shape, lambda *_: (0,) * len(shape), pipeline_mode=pl.Buffered(1))
    return pl.BlockSpec((None,) + tuple(shape), lambda *_: (layer,) + (0,) * len(shape), pipeline_mode=pl.Buffered(1))


def _in_proj_lru_kernel(x_ref, g_ref, w_ref, cs_ref, cb_ref, cw_ref, cvb_ref, wa_ref, ba_ref, wx_ref, bx_ref, lam_ref,
                        ya_ref, lat_ref, fox_ref, gate_ref, ext_scr, hc_scr, *, widths, tm):
    W = LRU_WIDTH
    pad = SUBLANES

    @pl.when(pl.program_id(1) == 0)
    def _():
        ext_scr[0:pad, :] = jnp.zeros((pad, W), F32)
        hc_scr[...] = jnp.zeros((1, W), F32)

    h = _rms(x_ref[...], g_ref[...]).astype(BF16)

    def tile(j):
        sl = slice(j * IN_TN, (j + 1) * IN_TN)
        return jnp.dot(h, w_ref[:, sl], preferred_element_type=F32), sl

    ntu = widths[0] // IN_TN
    uu = jnp.concatenate([tile(j)[0] for j in range(ntu)], axis=1)
    u, ug = uu[:, 0:W], uu[:, W:2 * W]
    ext_scr[pad:pad + tm, :] = u
    xc = cvb_ref[...] + ext_scr[pl.ds(pad - 3, tm), :] * cw_ref[0:1, :]
    xc = xc + ext_scr[pl.ds(pad - 2, tm), :] * cw_ref[1:2, :]
    xc = xc + ext_scr[pl.ds(pad - 1, tm), :] * cw_ref[2:3, :]
    xc = xc + u * cw_ref[3:4, :]
    ext_scr[0:pad, :] = u[tm - pad:tm, :]
    xb = xc.astype(BF16)
    r = _sigmoid(jnp.dot(xb, wa_ref[...], preferred_element_type=F32) + ba_ref[...])
    ig = _sigmoid(jnp.dot(xb, wx_ref[...], preferred_element_type=F32) + bx_ref[...])
    log_a = (-LRU_C) * r * _softplus(-lam_ref[...])
    a = jnp.exp(log_a)
    b = jnp.sqrt(jnp.tanh(-log_a) * (a * a + 1.0)) * (ig * xc)

    row = lax.broadcasted_iota(jnp.int32, (SUBLANES, W), 0)
    h_prev = hc_scr[...]
    hs = []
    for i in range(tm // SUBLANES):
        aa = a[i * SUBLANES:(i + 1) * SUBLANES, :]
        bb = b[i * SUBLANES:(i + 1) * SUBLANES, :]
        d = 1
        while d < SUBLANES:
            keep = row >= d
            a_sh = jnp.where(keep, pltpu.roll(aa, d, 0), 1.0)
            b_sh = jnp.where(keep, pltpu.roll(bb, d, 0), 0.0)
            bb = aa * b_sh + bb
            aa = aa * a_sh
            d *= 2
        hb = bb + aa * h_prev
        hs.append(hb)
        h_prev = hb[SUBLANES - 1:SUBLANES, :]
    hc_scr[...] = h_prev
    ya_ref[...] = (jnp.concatenate(hs, axis=0) * jax.nn.gelu(ug)).astype(ya_ref.dtype)

    col = widths[0]
    for slab, (o_ref, width) in enumerate(zip((lat_ref, fox_ref, gate_ref), widths[1:])):
        for j in range(width // IN_TN):
            acc, sl = tile(col // IN_TN + j)
            if slab == 1:
                acc = acc * cs_ref[:, sl]
            if slab == 2:
                acc = _sigmoid(acc + cb_ref[:, sl])
            o_ref[:, j * IN_TN:(j + 1) * IN_TN] = acc.astype(o_ref.dtype)
        col += width


def _in_proj_lru(x, g, w_all, cs, cb, widths, conv_w, conv_b, wa_bd, ba, wx_bd, bx, lam, layer, B, S, tm=512):
    D = x.shape[-1]
    N = sum(widths)
    W = LRU_WIDTH
    rowv = lambda v: v.reshape(1, W)
    out = lambda w: pl.BlockSpec((None, tm, w), lambda b, t: (b, t, 0))
    return pl.pallas_call(
        functools.partial(_in_proj_lru_kernel, widths=widths, tm=tm),
        grid=(B, S // tm),
        in_specs=[
            pl.BlockSpec((None, tm, D), lambda b, t: (b, t, 0)),
            _resident((1, D)), _resident((D, N), layer), _resident((1, N)), _resident((1, N)),
            _resident((CONV_WIDTH, W)), _resident((1, W)),
            _resident((W, W), layer), _resident((1, W)), _resident((W, W), layer), _resident((1, W)), _resident((1, W)),
        ],
        out_specs=[out(W), out(widths[1]), out(widths[2]), out(widths[3])],
        out_shape=[jax.ShapeDtypeStruct((B, S, W), BF16), jax.ShapeDtypeStruct((B, S, widths[1]), F32),
                   jax.ShapeDtypeStruct((B, S, widths[2]), BF16), jax.ShapeDtypeStruct((B, S, widths[3]), BF16)],
        scratch_shapes=[pltpu.VMEM((tm + SUBLANES, W), F32), pltpu.VMEM((1, W), F32)],
        compiler_params=_cparams(("parallel", "arbitrary")),
        name="in_proj_lru",
    )(x.reshape(B, S, D), g.reshape(1, D), w_all, cs, cb, conv_w, rowv(conv_b), wa_bd, rowv(ba), wx_bd, rowv(bx),
      rowv(lam))


def _rope_slot(x, c, s1, s2):
    return x * c + pltpu.roll(x, LANES - MLA_ROPE // 2, 1) * s1 + pltpu.roll(x, MLA_ROPE // 2, 1) * s2


def _bf16_terms(c):
    hi = c.astype(BF16).astype(F32)
    r = c - hi
    lo = r.astype(BF16).astype(F32)
    return hi, lo, (r - lo).astype(BF16).astype(F32)


def _fox_bias_tables():
    rows = jnp.arange(3 * LANES)
    term, head = rows // LANES, rows % LANES - FLOGIT_LANE0
    valid = (head >= 0) & (head < FOX_HEADS)
    cols = jnp.arange(FOX_HEADS * HEAD_SLOT)
    col_head, col_lane = cols // HEAD_SLOT, cols % HEAD_SLOT - FOX_HEAD_DIM
    hit = valid[:, None] & (head[:, None] == col_head[None, :])
    sel_q = jnp.where(hit & (col_lane[None, :] == term[:, None]), 1.0, 0.0).astype(BF16)
    sel_k = jnp.where(hit & (col_lane[None, :] == term[:, None] + 3), -1.0, 0.0).astype(BF16)
    one_q = jnp.where((col_lane >= 3) & (col_lane < 6), 1.0, 0.0).astype(F32)[None, :]
    one_k = jnp.where((col_lane >= 0) & (col_lane < 3), 1.0, 0.0).astype(F32)[None, :]
    return sel_q, sel_k, one_q, one_k


def _store_values_t(vt_ref, v, ts):
    vt = v.T
    row = lax.broadcasted_iota(jnp.int32, (V_ROWS - MLA_V, ts), 0)
    ones_tile = jnp.where(row == 0, 1.0, 0.0).astype(vt_ref.dtype)
    for h in range(MLA_HEADS):
        vt_ref[h * V_ROWS:h * V_ROWS + MLA_V, :] = vt[h * MLA_V:(h + 1) * MLA_V, :].astype(vt_ref.dtype)
        vt_ref[h * V_ROWS + MLA_V:(h + 1) * V_ROWS, :] = ones_tile


def _prep_kernel(lat_ref, fox_ref, cq_t_ref, ck_t_ref, s1_ref, s2_ref, qn_ref, kvn_ref, wq_ref, wk_ref, wv_ref,
                 bf_ref, selq_ref, selk_ref, oneq_ref, onek_ref, qm_ref, km_ref, vtm_ref, qf_ref, kf_ref, vtf_ref,
                 carry_scr, *, ts, q_scale):
    @pl.when(pl.program_id(1) == 0)
    def _():
        carry_scr[...] = jnp.zeros((1, LANES), F32)

    cq_t, ck_t, s1, s2 = cq_t_ref[...], ck_t_ref[...], s1_ref[...], s2_ref[...]

    cqn = _rms(lat_ref[:, 0:MLA_Q_LORA], qn_ref[...]).astype(BF16)
    q = jnp.dot(cqn, wq_ref[...], preferred_element_type=F32)
    for h in range(MLA_HEADS):
        sl = slice(h * HEAD_SLOT, (h + 1) * HEAD_SLOT)
        qm_ref[:, sl] = (_rope_slot(q[:, sl], cq_t, s1, s2) * q_scale).astype(qm_ref.dtype)

    ckvn = _rms(lat_ref[:, MLA_Q_LORA:MLA_Q_LORA + MLA_KV_LORA], kvn_ref[...]).astype(BF16)
    kn = jnp.dot(ckvn, wk_ref[...], preferred_element_type=F32)
    misc = lat_ref[:, MLA_Q_LORA + MLA_KV_LORA:MLA_Q_LORA + MLA_KV_LORA + LANES]
    kr = _rope_slot(misc, ck_t, s1, s2)
    for h in range(MLA_HEADS):
        sl = slice(h * HEAD_SLOT, (h + 1) * HEAD_SLOT)
        km_ref[:, sl] = (kn[:, sl] + kr).astype(km_ref.dtype)
    _store_values_t(vtm_ref, jnp.dot(ckvn, wv_ref[...], preferred_element_type=F32), ts)

    c = -_softplus(-(misc + bf_ref[...]))
    row = lax.broadcasted_iota(jnp.int32, (ts, LANES), 0)
    d = 1
    while d < ts:
        c = c + jnp.where(row >= d, pltpu.roll(c, d, 0), 0.0)
        d *= 2
    c = c + carry_scr[...]
    carry_scr[...] = c[ts - 1:ts, :]
    c = c * LOG2E

    terms = jnp.concatenate(_bf16_terms(c), axis=1).astype(BF16)
    nslot = FOX_HEADS * HEAD_SLOT
    q_bias = jnp.dot(terms, selq_ref[...], preferred_element_type=F32) + oneq_ref[...]
    k_bias = jnp.dot(terms, selk_ref[...], preferred_element_type=F32) + onek_ref[...]
    qf_ref[...] = (fox_ref[:, 0:nslot].astype(F32) + q_bias).astype(qf_ref.dtype)
    kf_ref[...] = (fox_ref[:, nslot:2 * nslot].astype(F32) + k_bias).astype(kf_ref.dtype)
    _store_values_t(vtf_ref, fox_ref[:, 2 * nslot:2 * nslot + FOX_WIDTH].astype(F32), ts)


def _prep(lat, fox, tabs, qn, kvn, wq, wk, wv, bf_row, bias_tabs, B, S, q_scale, ts=1024):
    NL, NF = lat.shape[-1], fox.shape[-1]
    full = lambda shape: pl.BlockSpec(shape, lambda b, t: (0,) * len(shape))
    tab = pl.BlockSpec((ts, LANES), lambda b, t: (t, 0))
    HS = MLA_HEADS * HEAD_SLOT
    slots = pl.BlockSpec((None, ts, HS), lambda b, t: (b, t, 0))
    vts = pl.BlockSpec((None, MLA_HEADS * V_ROWS, ts), lambda b, t: (b, 0, t))
    slot_shape = jax.ShapeDtypeStruct((B, S, HS), BF16)
    vt_shape = jax.ShapeDtypeStruct((B, MLA_HEADS * V_ROWS, S), BF16)
    return pl.pallas_call(
        functools.partial(_prep_kernel, ts=ts, q_scale=q_scale),
        grid=(B, S // ts),
        in_specs=[
            pl.BlockSpec((None, ts, NL), lambda b, t: (b, t, 0)),
            pl.BlockSpec((None, ts, NF), lambda b, t: (b, t, 0)),
            tab, tab, tab, tab,
            full((1, MLA_Q_LORA)), full((1, MLA_KV_LORA)),
            full((MLA_Q_LORA, HS)), full((MLA_KV_LORA, HS)), full((MLA_KV_LORA, MLA_HEADS * MLA_V)),
            full((1, LANES)),
            full((3 * LANES, HS)), full((3 * LANES, HS)), full((1, HS)), full((1, HS)),
        ],
        out_specs=[slots, slots, vts, slots, slots, vts],
        out_shape=[slot_shape, slot_shape, vt_shape, slot_shape, slot_shape, vt_shape],
        scratch_shapes=[pltpu.VMEM((1, LANES), F32)],
        compiler_params=_cparams(("parallel", "arbitrary")),
        name="mla_fox_prep",
    )(lat.reshape(B, S, NL), fox.reshape(B, S, NF), *tabs, qn.reshape(1, -1), kvn.reshape(1, -1), wq, wk, wv,
      bf_row, *bias_tabs)


def _attn_kernel(q_ref, k_ref, vt_ref, mask_ref, o_ref, m_scr, acc_scr, s_scr, *, tq, tk):
    qi = pl.program_id(2)
    nsub = tq // Q_SUB
    ndiag = tq // tk
    m_scr[...] = jnp.full(m_scr.shape, NEG_INF, F32)
    acc_scr[...] = jnp.zeros(acc_scr.shape, F32)

    def diag_kinds(d):
        kinds = []
        for c in range(nsub):
            if (c + 1) * Q_SUB <= d * tk:
                kinds.append(None)
            elif (d + 1) * tk <= c * Q_SUB:
                kinds.append((tk, None))
            else:
                kinds.append((min(tk, (c + 1) * Q_SUB - d * tk), d * tk))
        return kinds

    def units(k0, kinds, slot0):
        live = [(h, c) for c in range(nsub) for h in range(2) if kinds[c] is not None]
        return [(k0, kinds[c][0], kinds[c][1], h, c, slot0 + idx) for idx, (h, c) in enumerate(live)]

    def logits_one(k0, klen, h, c, slot):
        kh = k_ref[pl.ds(k0, klen), h * HEAD_SLOT:(h + 1) * HEAD_SLOT]
        qh = q_ref[c * Q_SUB:(c + 1) * Q_SUB, h * HEAD_SLOT:(h + 1) * HEAD_SLOT]
        s_scr[slot, 0:klen] = lax.dot_general(kh, qh, (((1,), (1,)), ((), ())), preferred_element_type=F32)

    def softmax_one(k0, klen, mask_off, h, c, slot):
        st = s_scr[slot, 0:klen]
        cs = slice(c * Q_SUB, (c + 1) * Q_SUB)
        if mask_off is not None:
            st = st + mask_ref[(c * Q_SUB - mask_off) // Q_SUB, 0:klen, :]
        m_prev = m_scr[h, :, cs]
        m_new = jnp.maximum(m_prev, jnp.max(st, axis=0, keepdims=True))
        alpha = jnp.exp2(m_prev - m_new)
        pt = jnp.exp2(st - m_new).astype(BF16)
        m_scr[h, :, cs] = m_new
        vth = vt_ref[h * V_ROWS:(h + 1) * V_ROWS, pl.ds(k0, klen)]
        acc_scr[h, :, cs] = alpha * acc_scr[h, :, cs] + jnp.dot(vth, pt, preferred_element_type=F32)

    def run(todo, ahead):
        for j in range(len(todo)):
            if j < len(ahead):
                k0, klen, _, h, c, slot = ahead[j]
                logits_one(k0, klen, h, c, slot)
            softmax_one(*todo[j])

    n = qi * ndiag
    per = 2 * nsub
    all_visible = [(tk, None)] * nsub
    for k0, klen, _, h, c, slot in units(0, all_visible, 0)[:SKEW]:
        logits_one(k0, klen, h, c, slot)

    def body(i, carry):
        chunks = [units(pl.multiple_of((LOOP_CHUNKS * i + t) * tk, tk), all_visible, (t % 2) * per)
                  for t in range(LOOP_CHUNKS + 1)]
        flat = sum(chunks, [])
        run(flat[:LOOP_CHUNKS * per], flat[SKEW:SKEW + LOOP_CHUNKS * per])
        return carry

    lax.fori_loop(0, n // LOOP_CHUNKS, body, 0)
    flat = []
    for d in range(ndiag):
        flat += units(pl.multiple_of(qi * tq + d * tk, tk), diag_kinds(d), (d % 2) * per)
    run(flat, flat[SKEW:])

    outs = []
    for h in range(2):
        a = acc_scr[h]
        outs.append(a[0:MLA_V, :] / a[MLA_V:MLA_V + 1, :])
    o_ref[...] = jnp.concatenate(outs, axis=0).T.astype(o_ref.dtype)


def _causal_bias(unit, tk):
    key = jnp.arange(tk)[None, :, None] // unit
    qry = (jnp.arange(Q_SUB)[None, None, :] + Q_SUB * jnp.arange(tk // Q_SUB)[:, None, None]) // unit
    return jnp.where(qry >= key, 0.0, NEG_INF).astype(F32)


def _attention(q, k, vt, B, S, *, unit, name, tq=2048, tk=512):
    npair = MLA_HEADS // 2
    nstream = 2 * (tq // Q_SUB)
    assert (tq // tk) % LOOP_CHUNKS == 0 and LOOP_CHUNKS % 2 == 0
    return pl.pallas_call(
        functools.partial(_attn_kernel, tq=tq, tk=tk),
        grid=(B, npair, S // tq),
        in_specs=[
            pl.BlockSpec((None, tq, 2 * HEAD_SLOT), lambda b, p, i: (b, i, p)),
            pl.BlockSpec((None, S, 2 * HEAD_SLOT), lambda b, p, i: (b, 0, p)),
            pl.BlockSpec((None, 2 * V_ROWS, S), lambda b, p, i: (b, p, 0)),
            pl.BlockSpec((tk // Q_SUB, tk, Q_SUB), lambda b, p, i: (0, 0, 0)),
        ],
        out_specs=pl.BlockSpec((None, tq, 2 * MLA_V), lambda b, p, i: (b, i, p)),
        out_shape=jax.ShapeDtypeStruct((B, S, npair * 2 * MLA_V), BF16),
        scratch_shapes=[pltpu.VMEM((2, 1, tq), F32), pltpu.VMEM((2, V_ROWS, tq), F32),
                        pltpu.VMEM((2 * nstream, tk, Q_SUB), F32)],
        compiler_params=_cparams(("parallel", "parallel", "arbitrary")),
        name=name,
    )(q, k, vt, _causal_bias(unit, tk))


def _merge_kernel(ya_ref, ob_ref, oc_ref, g_ref, x_ref, wa_ref, wb_ref, wc_ref, wo_ref, o_ref):
    D = x_ref.shape[-1]
    ya = jnp.dot(ya_ref[...], wa_ref[...], preferred_element_type=F32)
    yb = jnp.dot(ob_ref[...], wb_ref[...], preferred_element_type=F32)
    yc = jnp.dot(oc_ref[...], wc_ref[...], preferred_element_type=F32)
    merged = g_ref[:, 0:D] * ya + g_ref[:, D:2 * D] * yb + g_ref[:, 2 * D:3 * D] * yc
    o_ref[...] = x_ref[...] + jnp.dot(merged.astype(BF16), wo_ref[...], preferred_element_type=F32)


def _merge(ya, ob, oc, g, x, wa, wb, wc, wo, layer, tm=512):
    T, D = x.shape
    W = ya.shape[-1]
    rows = lambda n: pl.BlockSpec((tm, n), lambda i: (i, 0))
    return pl.pallas_call(
        _merge_kernel,
        grid=(T // tm,),
        in_specs=[rows(W), rows(W), rows(W), rows(3 * D), rows(D),
                  _resident((W, D), layer), _resident((W, D), layer), _resident((W, D), layer),
                  _resident((D, D), layer)],
        out_specs=rows(D),
        out_shape=jax.ShapeDtypeStruct((T, D), F32),
        compiler_params=_cparams(("parallel",)),
        name="merge",
    )(ya, ob, oc, g, x, wa, wb, wc, wo)


FFN_TF = 1408


def _ffn_ple_kernel(x_ref, p_ref, fg_ref, wgu_ref, wd_ref, pn_ref, wpg_ref, wple_ref, fn_ref, o_ref, *, final):
    d_ff = wd_ref.shape[0]
    x = x_ref[...]
    h = _rms(x, fg_ref[...]).astype(BF16)
    for c in range(d_ff // FFN_TF):
        gate = jnp.dot(h, wgu_ref[:, c * FFN_TF:(c + 1) * FFN_TF], preferred_element_type=F32)
        up = jnp.dot(h, wgu_ref[:, d_ff + c * FFN_TF:d_ff + (c + 1) * FFN_TF], preferred_element_type=F32)
        act = (gate * _sigmoid(gate) * up).astype(BF16)
        x = x + jnp.dot(act, wd_ref[c * FFN_TF:(c + 1) * FFN_TF, :], preferred_element_type=F32)
    hn = _rms(x, pn_ref[...]).astype(BF16)
    pg = _sigmoid(jnp.dot(hn, wpg_ref[...], preferred_element_type=F32))
    pe = jnp.dot(p_ref[...].astype(BF16), wple_ref[...], preferred_element_type=F32)
    x = x + pg * pe
    if final:
        x = _rms(x, fn_ref[...])
    o_ref[...] = x


def _ffn_ple(x, p, layer, fg, w_gate_up, w_down, pn, wpg, wple, fn, final, tm=512):
    T, D = x.shape
    P = p.shape[-1]
    d_ff = w_down.shape[1]
    assert d_ff % FFN_TF == 0
    return pl.pallas_call(
        functools.partial(_ffn_ple_kernel, final=final),
        grid=(T // tm,),
        in_specs=[pl.BlockSpec((tm, D), lambda i: (i, 0)), pl.BlockSpec((None, tm, P), lambda i: (layer, i, 0)),
                  _resident((1, D)), _resident((D, 2 * d_ff), layer), _resident((d_ff, D), layer),
                  _resident((1, D)), _resident((D, D), layer), _resident((P, D), layer), _resident((1, D))],
        out_specs=pl.BlockSpec((tm, D), lambda i: (i, 0)),
        out_shape=jax.ShapeDtypeStruct((T, D), F32),
        compiler_params=_cparams(("parallel",)),
        name="ffn_ple_final" if final else "ffn_ple",
    )(x, p, fg.reshape(1, D), w_gate_up, w_down, pn.reshape(1, D), wpg, wple, fn.reshape(1, D))


def _block_diag(w):
    L, H, I, J = w.shape
    eye = jnp.eye(H, dtype=w.dtype)
    return (eye[None, :, None, :, None] * w[:, :, :, None, :]).reshape(L, H * I, H * J)


def _head_slots(w, width):
    lead = w.shape[:-1]
    w = w.reshape(*lead, MLA_HEADS, width)
    pad = [(0, 0)] * (w.ndim - 1) + [(0, HEAD_SLOT - width)]
    return jnp.pad(w, pad).reshape(*lead, MLA_HEADS * HEAD_SLOT)


def _rope_tables(S):
    pos = jnp.arange(S, dtype=F32)
    inv_freq = ROPE_BASE ** (-jnp.arange(0, MLA_ROPE, 2, dtype=F32) / MLA_ROPE)
    ang = pos[:, None] * inv_freq[None, :]
    cos, sin = jnp.cos(ang), jnp.sin(ang)
    half = MLA_ROPE // 2
    z = lambda n: jnp.zeros((S, n), F32)
    tail = LANES - ROPE_LANE0 - MLA_ROPE
    cq_t = jnp.concatenate([jnp.ones((S, ROPE_LANE0), F32), cos, cos, z(tail)], axis=1)
    ck_t = jnp.concatenate([z(ROPE_LANE0), cos, cos, z(tail)], axis=1)
    s1 = jnp.concatenate([z(ROPE_LANE0), -sin, z(half), z(tail)], axis=1)
    s2 = jnp.concatenate([z(ROPE_LANE0), z(half), sin, z(tail)], axis=1)
    return cq_t, ck_t, s1, s2


def kernel(x, p, mix_norm, w_in, gate_b, conv_w, conv_b, lru_wa, lru_ba, lru_wx, lru_bx, lru_lambda,
           mla_q_norm, mla_wuq, mla_kv_norm, mla_wukv, fox_bf, w_br_a, w_br_b, w_br_c, w_o,
           ffn_norm, w_gate_up, w_down, ple_norm, w_ple_gate, w_ple, final_norm):
    B, S, D = x.shape
    depth = w_in.shape[0]
    T = B * S
    tabs = _rope_tables(S)
    bias_tabs = _fox_bias_tables()
    mla_scale = (MLA_NOPE + MLA_ROPE) ** -0.5 * LOG2E
    fox_scale = FOX_HEAD_DIM ** -0.5 * LOG2E

    o_u = 0
    o_cq = 2 * LRU_WIDTH
    o_ckv = o_cq + MLA_Q_LORA
    o_kr = o_ckv + MLA_KV_LORA
    o_fq = o_kr + MLA_ROPE
    o_fl = o_fq + 3 * FOX_WIDTH
    o_g = o_fl + FOX_HEADS

    wb = w_in.astype(BF16)
    zcols = lambda n: jnp.zeros((depth, D, n), BF16)
    lat_w = MLA_Q_LORA + MLA_KV_LORA + LANES
    lat_pad = -lat_w % IN_TN
    nslot = FOX_HEADS * HEAD_SLOT
    w_all = jnp.concatenate([
        wb[..., o_u:o_cq],
        wb[..., o_cq:o_kr], zcols(ROPE_LANE0), wb[..., o_kr:o_fq], wb[..., o_fl:o_g],
        zcols(LANES - ROPE_LANE0 - MLA_ROPE - FOX_HEADS + lat_pad),
        _head_slots(wb[..., o_fq:o_fq + FOX_WIDTH], FOX_HEAD_DIM),
        _head_slots(wb[..., o_fq + FOX_WIDTH:o_fq + 2 * FOX_WIDTH], FOX_HEAD_DIM),
        wb[..., o_fq + 2 * FOX_WIDTH:o_fl],
        wb[..., o_g:]], axis=-1)
    widths = (2 * LRU_WIDTH, lat_w + lat_pad, 2 * nslot + FOX_WIDTH, 3 * D)
    n_before_fox, n_before_gate = widths[0] + widths[1], widths[0] + widths[1] + widths[2]
    cs = jnp.ones((1, sum(widths)), F32).at[0, n_before_fox:n_before_fox + nslot].set(fox_scale)
    cb = jnp.concatenate([jnp.zeros((depth, n_before_gate), F32), gate_b], axis=1)[:, None, :]
    wa_bd, wx_bd = _block_diag(lru_wa).astype(BF16), _block_diag(lru_wx).astype(BF16)
    wq = _head_slots(mla_wuq.astype(BF16), MLA_NOPE + MLA_ROPE)
    wukv = mla_wukv.astype(BF16).reshape(depth, MLA_KV_LORA, MLA_HEADS, MLA_NOPE + MLA_V)
    wk = _head_slots(wukv[..., :MLA_NOPE].reshape(depth, MLA_KV_LORA, MLA_HEADS * MLA_NOPE), MLA_NOPE)
    wv = wukv[..., MLA_NOPE:].reshape(depth, MLA_KV_LORA, MLA_HEADS * MLA_V)
    bf_rows = jnp.zeros((depth, 1, LANES), F32).at[:, 0, FLOGIT_LANE0:FLOGIT_LANE0 + FOX_HEADS].set(fox_bf)
    wbr_a, wbr_b, wbr_c, wo = (w.astype(BF16) for w in (w_br_a, w_br_b, w_br_c, w_o))
    wgu, wdn, wpg, wpl = (w.astype(BF16) for w in (w_gate_up, w_down, w_ple_gate, w_ple))

    xf = x.reshape(T, D)
    for i in range(depth):
        ya, lat, fox, g = _in_proj_lru(xf, mix_norm[i], w_all, cs, cb[i], widths, conv_w[i], conv_b[i], wa_bd,
                                       lru_ba[i], wx_bd, lru_bx[i], lru_lambda[i], i, B, S)
        q_m, k_m, vt_m, q_f, k_f, vt_f = _prep(lat, fox, tabs, mla_q_norm[i], mla_kv_norm[i], wq[i], wk[i], wv[i],
                                               bf_rows[i], bias_tabs, B, S, mla_scale)
        ob = _attention(q_m, k_m, vt_m, B, S, unit=MLA_CHUNK, name="mla_attention")
        oc = _attention(q_f, k_f, vt_f, B, S, unit=1, name="fox_attention")
        xf = _merge(ya.reshape(T, -1), ob.reshape(T, -1), oc.reshape(T, -1), g.reshape(T, -1), xf,
                    wbr_a, wbr_b, wbr_c, wo, i)
        xf = _ffn_ple(xf, p.reshape(depth, T, -1), i, ffn_norm[i], wgu, wdn, ple_norm[i], wpg, wpl,
                      final_norm, final=(i == depth - 1))
    return xf.reshape(B, S, D)
```

```python
import functools
import math

import jax
import jax.numpy as jnp
from jax import lax
from jax.experimental import pallas as pl
from jax.experimental.pallas import tpu as pltpu

F32 = jnp.float32
BF16 = jnp.bfloat16

EPS = 1e-6
NEG_INF = -1e30
LOG2E = math.log2(math.e)

LRU_WIDTH = 512
LRU_HEADS = 8
CONV_WIDTH = 4
LRU_C = 8.0

MLA_HEADS = 8
MLA_Q_LORA = 384
MLA_KV_LORA = 256
MLA_NOPE = 64
MLA_ROPE = 32
MLA_V = 64
ROPE_BASE = 10000.0
MLA_CHUNK = 64

FOX_HEADS = 8
FOX_HEAD_DIM = 64
FOX_WIDTH = FOX_HEADS * FOX_HEAD_DIM

LANES = 128
SUBLANES = 8
HEAD_SLOT = LANES
ROPE_LANE0 = MLA_NOPE
FLOGIT_LANE0 = MLA_NOPE + MLA_ROPE
V_ROWS = 80
Q_SUB = 256
SKEW = 4
LOOP_CHUNKS = 4

VMEM_LIMIT = 56 * 1024 * 1024


def _cparams(sem):
    return pltpu.CompilerParams(dimension_semantics=sem, vmem_limit_bytes=VMEM_LIMIT)


def _rms(xf, g):
    ms = jnp.mean(xf * xf, axis=-1, keepdims=True)
    return xf * lax.rsqrt(ms + EPS) * g


def _sigmoid(z):
    return 0.5 * jnp.tanh(0.5 * z) + 0.5


def _softplus(z):
    return jnp.maximum(z, 0.0) + jnp.log1p(jnp.exp(-jnp.abs(z)))


IN_TN = 256


def _resident(shape, layer=None):
    if layer is None:
        return pl.BlockSpec(shape, lambda *_: (0,) * len(shape), pipeline_mode=pl.Buffered(1))
    return pl.BlockSpec((None,) + tuple(shape), lambda *_: (layer,) + (0,) * len(shape), pipeline_mode=pl.Buffered(1))


def _in_proj_lru_kernel(x_ref, g_ref, w_ref, cs_ref, cb_ref, cw_ref, cvb_ref, wa_ref, ba_ref, wx_ref, bx_ref, lam_ref,
                        ya_ref, lat_ref, fox_ref, gate_ref, ext_scr, hc_scr, *, widths, tm):
    W = LRU_WIDTH
    pad = SUBLANES

    @pl.when(pl.program_id(1) == 0)
    def _():
        ext_scr[0:pad, :] = jnp.zeros((pad, W), F32)
        hc_scr[...] = jnp.zeros((1, W), F32)

    h = _rms(x_ref[...], g_ref[...]).astype(BF16)

    def tile(j):
        sl = slice(j * IN_TN, (j + 1) * IN_TN)
        return jnp.dot(h, w_ref[:, sl], preferred_element_type=F32), sl

    ntu = widths[0] // IN_TN
    uu = jnp.concatenate([tile(j)[0] for j in range(ntu)], axis=1)
    u, ug = uu[:, 0:W], uu[:, W:2 * W]
    ext_scr[pad:pad + tm, :] = u
    xc = cvb_ref[...] + ext_scr[pl.ds(pad - 3, tm), :] * cw_ref[0:1, :]
    xc = xc + ext_scr[pl.ds(pad - 2, tm), :] * cw_ref[1:2, :]
    xc = xc + ext_scr[pl.ds(pad - 1, tm), :] * cw_ref[2:3, :]
    xc = xc + u * cw_ref[3:4, :]
    ext_scr[0:pad, :] = u[tm - pad:tm, :]
    xb = xc.astype(BF16)
    r = _sigmoid(jnp.dot(xb, wa_ref[...], preferred_element_type=F32) + ba_ref[...])
    ig = _sigmoid(jnp.dot(xb, wx_ref[...], preferred_element_type=F32) + bx_ref[...])
    log_a = (-LRU_C) * r * _softplus(-lam_ref[...])
    a = jnp.exp(log_a)
    b = jnp.sqrt(jnp.tanh(-log_a) * (a * a + 1.0)) * (ig * xc)

    row = lax.broadcasted_iota(jnp.int32, (SUBLANES, W), 0)
    h_prev = hc_scr[...]
    hs = []
    for i in range(tm // SUBLANES):
        aa = a[i * SUBLANES:(i + 1) * SUBLANES, :]
        bb = b[i * SUBLANES:(i + 1) * SUBLANES, :]
        d = 1
        while d < SUBLANES:
            keep = row >= d
            a_sh = jnp.where(keep, pltpu.roll(aa, d, 0), 1.0)
            b_sh = jnp.where(keep, pltpu.roll(bb, d, 0), 0.0)
            bb = aa * b_sh + bb
            aa = aa * a_sh
            d *= 2
        hb = bb + aa * h_prev
        hs.append(hb)
        h_prev = hb[SUBLANES - 1:SUBLANES, :]
    hc_scr[...] = h_prev
    ya_ref[...] = (jnp.concatenate(hs, axis=0) * jax.nn.gelu(ug)).astype(ya_ref.dtype)

    col = widths[0]
    for slab, (o_ref, width) in enumerate(zip((lat_ref, fox_ref, gate_ref), widths[1:])):
        for j in range(width // IN_TN):
            acc, sl = tile(col // IN_TN + j)
            if slab == 1:
                acc = acc * cs_ref[:, sl]
            if slab == 2:
                acc = _sigmoid(acc + cb_ref[:, sl])
            o_ref[:, j * IN_TN:(j + 1) * IN_TN] = acc.astype(o_ref.dtype)
        col += width


def _in_proj_lru(x, g, w_all, cs, cb, widths, conv_w, conv_b, wa_bd, ba, wx_bd, bx, lam, layer, B, S, tm=512):
    D = x.shape[-1]
    N = sum(widths)
    W = LRU_WIDTH
    rowv = lambda v: v.reshape(1, W)
    out = lambda w: pl.BlockSpec((None, tm, w), lambda b, t: (b, t, 0))
    return pl.pallas_call(
        functools.partial(_in_proj_lru_kernel, widths=widths, tm=tm),
        grid=(B, S // tm),
        in_specs=[
            pl.BlockSpec((None, tm, D), lambda b, t: (b, t, 0)),
            _resident((1, D)), _resident((D, N), layer), _resident((1, N)), _resident((1, N)),
            _resident((CONV_WIDTH, W)), _resident((1, W)),
            _resident((W, W), layer), _resident((1, W)), _resident((W, W), layer), _resident((1, W)), _resident((1, W)),
        ],
        out_specs=[out(W), out(widths[1]), out(widths[2]), out(widths[3])],
        out_shape=[jax.ShapeDtypeStruct((B, S, W), BF16), jax.ShapeDtypeStruct((B, S, widths[1]), F32),
                   jax.ShapeDtypeStruct((B, S, widths[2]), BF16), jax.ShapeDtypeStruct((B, S, widths[3]), BF16)],
        scratch_shapes=[pltpu.VMEM((tm + SUBLANES, W), F32), pltpu.VMEM((1, W), F32)],
        compiler_params=_cparams(("parallel", "arbitrary")),
        name="in_proj_lru",
    )(x.reshape(B, S, D), g.reshape(1, D), w_all, cs, cb, conv_w, rowv(conv_b), wa_bd, rowv(ba), wx_bd, rowv(bx),
      rowv(lam))


def _rope_slot(x, c, s1, s2):
    return x * c + pltpu.roll(x, LANES - MLA_ROPE // 2, 1) * s1 + pltpu.roll(x, MLA_ROPE // 2, 1) * s2


def _bf16_terms(c):
    hi = c.astype(BF16).astype(F32)
    r = c - hi
    lo = r.astype(BF16).astype(F32)
    return hi, lo, (r - lo).astype(BF16).astype(F32)


def _fox_bias_tables():
    rows = jnp.arange(3 * LANES)
    term, head = rows // LANES, rows % LANES - FLOGIT_LANE0
    valid = (head >= 0) & (head < FOX_HEADS)
    cols = jnp.arange(FOX_HEADS * HEAD_SLOT)
    col_head, col_lane = cols // HEAD_SLOT, cols % HEAD_SLOT - FOX_HEAD_DIM
    hit = valid[:, None] & (head[:, None] == col_head[None, :])
    sel_q = jnp.where(hit & (col_lane[None, :] == term[:, None]), 1.0, 0.0).astype(BF16)
    sel_k = jnp.where(hit & (col_lane[None, :] == term[:, None] + 3), -1.0, 0.0).astype(BF16)
    one_q = jnp.where((col_lane >= 3) & (col_lane < 6), 1.0, 0.0).astype(F32)[None, :]
    one_k = jnp.where((col_lane >= 0) & (col_lane < 3), 1.0, 0.0).astype(F32)[None, :]
    return sel_q, sel_k, one_q, one_k


def _store_values_t(vt_ref, v, ts):
    vt = v.T
    row = lax.broadcasted_iota(jnp.int32, (V_ROWS - MLA_V, ts), 0)
    ones_tile = jnp.where(row == 0, 1.0, 0.0).astype(vt_ref.dtype)
    for h in range(MLA_HEADS):
        vt_ref[h * V_ROWS:h * V_ROWS + MLA_V, :] = vt[h * MLA_V:(h + 1) * MLA_V, :].astype(vt_ref.dtype)
        vt_ref[h * V_ROWS + MLA_V:(h + 1) * V_ROWS, :] = ones_tile


def _prep_kernel(lat_ref, fox_ref, cq_t_ref, ck_t_ref, s1_ref, s2_ref, qn_ref, kvn_ref, wq_ref, wk_ref, wv_ref,
                 bf_ref, selq_ref, selk_ref, oneq_ref, onek_ref, qm_ref, km_ref, vtm_ref, qf_ref, kf_ref, vtf_ref,
                 carry_scr, *, ts, q_scale):
    @pl.when(pl.program_id(1) == 0)
    def _():
        carry_scr[...] = jnp.zeros((1, LANES), F32)

    cq_t, ck_t, s1, s2 = cq_t_ref[...], ck_t_ref[...], s1_ref[...], s2_ref[...]

    cqn = _rms(lat_ref[:, 0:MLA_Q_LORA], qn_ref[...]).astype(BF16)
    q = jnp.dot(cqn, wq_ref[...], preferred_element_type=F32)
    for h in range(MLA_HEADS):
        sl = slice(h * HEAD_SLOT, (h + 1) * HEAD_SLOT)
        qm_ref[:, sl] = (_rope_slot(q[:, sl], cq_t, s1, s2) * q_scale).astype(qm_ref.dtype)

    ckvn = _rms(lat_ref[:, MLA_Q_LORA:MLA_Q_LORA + MLA_KV_LORA], kvn_ref[...]).astype(BF16)
    kn = jnp.dot(ckvn, wk_ref[...], preferred_element_type=F32)
    misc = lat_ref[:, MLA_Q_LORA + MLA_KV_LORA:MLA_Q_LORA + MLA_KV_LORA + LANES]
    kr = _rope_slot(misc, ck_t, s1, s2)
    for h in range(MLA_HEADS):
        sl = slice(h * HEAD_SLOT, (h + 1) * HEAD_SLOT)
        km_ref[:, sl] = (kn[:, sl] + kr).astype(km_ref.dtype)
    _store_values_t(vtm_ref, jnp.dot(ckvn, wv_ref[...], preferred_element_type=F32), ts)

    c = -_softplus(-(misc + bf_ref[...]))
    row = lax.broadcasted_iota(jnp.int32, (ts, LANES), 0)
    d = 1
    while d < ts:
        c = c + jnp.where(row >= d, pltpu.roll(c, d, 0), 0.0)
        d *= 2
    c = c + carry_scr[...]
    carry_scr[...] = c[ts - 1:ts, :]
    c = c * LOG2E

    terms = jnp.concatenate(_bf16_terms(c), axis=1).astype(BF16)
    nslot = FOX_HEADS * HEAD_SLOT
    q_bias = jnp.dot(terms, selq_ref[...], preferred_element_type=F32) + oneq_ref[...]
    k_bias = jnp.dot(terms, selk_ref[...], preferred_element_type=F32) + onek_ref[...]
    qf_ref[...] = (fox_ref[:, 0:nslot].astype(F32) + q_bias).astype(qf_ref.dtype)
    kf_ref[...] = (fox_ref[:, nslot:2 * nslot].astype(F32) + k_bias).astype(kf_ref.dtype)
    _store_values_t(vtf_ref, fox_ref[:, 2 * nslot:2 * nslot + FOX_WIDTH].astype(F32), ts)


def _prep(lat, fox, tabs, qn, kvn, wq, wk, wv, bf_row, bias_tabs, B, S, q_scale, ts=1024):
    NL, NF = lat.shape[-1], fox.shape[-1]
    full = lambda shape: pl.BlockSpec(shape, lambda b, t: (0,) * len(shape))
    tab = pl.BlockSpec((ts, LANES), lambda b, t: (t, 0))
    HS = MLA_HEADS * HEAD_SLOT
    slots = pl.BlockSpec((None, ts, HS), lambda b, t: (b, t, 0))
    vts = pl.BlockSpec((None, MLA_HEADS * V_ROWS, ts), lambda b, t: (b, 0, t))
    slot_shape = jax.ShapeDtypeStruct((B, S, HS), BF16)
    vt_shape = jax.ShapeDtypeStruct((B, MLA_HEADS * V_ROWS, S), BF16)
    return pl.pallas_call(
        functools.partial(_prep_kernel, ts=ts, q_scale=q_scale),
        grid=(B, S // ts),
        in_specs=[
            pl.BlockSpec((None, ts, NL), lambda b, t: (b, t, 0)),
            pl.BlockSpec((None, ts, NF), lambda b, t: (b, t, 0)),
            tab, tab, tab, tab,
            full((1, MLA_Q_LORA)), full((1, MLA_KV_LORA)),
            full((MLA_Q_LORA, HS)), full((MLA_KV_LORA, HS)), full((MLA_KV_LORA, MLA_HEADS * MLA_V)),
            full((1, LANES)),
            full((3 * LANES, HS)), full((3 * LANES, HS)), full((1, HS)), full((1, HS)),
        ],
        out_specs=[slots, slots, vts, slots, slots, vts],
        out_shape=[slot_shape, slot_shape, vt_shape, slot_shape, slot_shape, vt_shape],
        scratch_shapes=[pltpu.VMEM((1, LANES), F32)],
        compiler_params=_cparams(("parallel", "arbitrary")),
        name="mla_fox_prep",
    )(lat.reshape(B, S, NL), fox.reshape(B, S, NF), *tabs, qn.reshape(1, -1), kvn.reshape(1, -1), wq, wk, wv,
      bf_row, *bias_tabs)


def _attn_kernel(q_ref, k_ref, vt_ref, mask_ref, o_ref, m_scr, acc_scr, s_scr, *, tq, tk):
    qi = pl.program_id(2)
    nsub = tq // Q_SUB
    ndiag = tq // tk
    m_scr[...] = jnp.full(m_scr.shape, NEG_INF, F32)
    acc_scr[...] = jnp.zeros(acc_scr.shape, F32)

    def diag_kinds(d):
        kinds = []
        for c in range(nsub):
            if (c + 1) * Q_SUB <= d * tk:
                kinds.append(None)
            elif (d + 1) * tk <= c * Q_SUB:
                kinds.append((tk, None))
            else:
                kinds.append((min(tk, (c + 1) * Q_SUB - d * tk), d * tk))
        return kinds

    def units(k0, kinds, slot0):
        live = [(h, c) for c in range(nsub) for h in range(2) if kinds[c] is not None]
        return [(k0, kinds[c][0], kinds[c][1], h, c, slot0 + idx) for idx, (h, c) in enumerate(live)]

    def logits_one(k0, klen, h, c, slot):
        kh = k_ref[pl.ds(k0, klen), h * HEAD_SLOT:(h + 1) * HEAD_SLOT]
        qh = q_ref[c * Q_SUB:(c + 1) * Q_SUB, h * HEAD_SLOT:(h + 1) * HEAD_SLOT]
        s_scr[slot, 0:klen] = lax.dot_general(kh, qh, (((1,), (1,)), ((), ())), preferred_element_type=F32)

    def softmax_one(k0, klen, mask_off, h, c, slot):
        st = s_scr[slot, 0:klen]
        cs = slice(c * Q_SUB, (c + 1) * Q_SUB)
        if mask_off is not None:
            st = st + mask_ref[(c * Q_SUB - mask_off) // Q_SUB, 0:klen, :]
        m_prev = m_scr[h, :, cs]
        m_new = jnp.maximum(m_prev, jnp.max(st, axis=0, keepdims=True))
        alpha = jnp.exp2(m_prev - m_new)
        pt = jnp.exp2(st - m_new).astype(BF16)
        m_scr[h, :, cs] = m_new
        vth = vt_ref[h * V_ROWS:(h + 1) * V_ROWS, pl.ds(k0, klen)]
        acc_scr[h, :, cs] = alpha * acc_scr[h, :, cs] + jnp.dot(vth, pt, preferred_element_type=F32)

    def run(todo, ahead):
        for j in range(len(todo)):
            if j < len(ahead):
                k0, klen, _, h, c, slot = ahead[j]
                logits_one(k0, klen, h, c, slot)
            softmax_one(*todo[j])

    n = qi * ndiag
    per = 2 * nsub
    all_visible = [(tk, None)] * nsub
    for k0, klen, _, h, c, slot in units(0, all_visible, 0)[:SKEW]:
        logits_one(k0, klen, h, c, slot)

    def body(i, carry):
        chunks = [units(pl.multiple_of((LOOP_CHUNKS * i + t) * tk, tk), all_visible, (t % 2) * per)
                  for t in range(LOOP_CHUNKS + 1)]
        flat = sum(chunks, [])
        run(flat[:LOOP_CHUNKS * per], flat[SKEW:SKEW + LOOP_CHUNKS * per])
        return carry

    lax.fori_loop(0, n // LOOP_CHUNKS, body, 0)
    flat = []
    for d in range(ndiag):
        flat += units(pl.multiple_of(qi * tq + d * tk, tk), diag_kinds(d), (d % 2) * per)
    run(flat, flat[SKEW:])

    outs = []
    for h in range(2):
        a = acc_scr[h]
        outs.append(a[0:MLA_V, :] / a[MLA_V:MLA_V + 1, :])
    o_ref[...] = jnp.concatenate(outs, axis=0).T.astype(o_ref.dtype)


def _causal_bias(unit, tk):
    key = jnp.arange(tk)[None, :, None] // unit
    qry = (jnp.arange(Q_SUB)[None, None, :] + Q_SUB * jnp.arange(tk // Q_SUB)[:, None, None]) // unit
    return jnp.where(qry >= key, 0.0, NEG_INF).astype(F32)


def _attention(q, k, vt, B, S, *, unit, name, tq=2048, tk=512):
    npair = MLA_HEADS // 2
    nstream = 2 * (tq // Q_SUB)
    assert (tq // tk) % LOOP_CHUNKS == 0 and LOOP_CHUNKS % 2 == 0
    return pl.pallas_call(
        functools.partial(_attn_kernel, tq=tq, tk=tk),
        grid=(B, npair, S // tq),
        in_specs=[
            pl.BlockSpec((None, tq, 2 * HEAD_SLOT), lambda b, p, i: (b, i, p)),
            pl.BlockSpec((None, S, 2 * HEAD_SLOT), lambda b, p, i: (b, 0, p)),
            pl.BlockSpec((None, 2 * V_ROWS, S), lambda b, p, i: (b, p, 0)),
            pl.BlockSpec((tk // Q_SUB, tk, Q_SUB), lambda b, p, i: (0, 0, 0)),
        ],
        out_specs=pl.BlockSpec((None, tq, 2 * MLA_V), lambda b, p, i: (b, i, p)),
        out_shape=jax.ShapeDtypeStruct((B, S, npair * 2 * MLA_V), BF16),
        scratch_shapes=[pltpu.VMEM((2, 1, tq), F32), pltpu.VMEM((2, V_ROWS, tq), F32),
                        pltpu.VMEM((2 * nstream, tk, Q_SUB), F32)],
        compiler_params=_cparams(("parallel", "parallel", "arbitrary")),
        name=name,
    )(q, k, vt, _causal_bias(unit, tk))


def _merge_kernel(ya_ref, ob_ref, oc_ref, g_ref, x_ref, wa_ref, wb_ref, wc_ref, wo_ref, o_ref):
    D = x_ref.shape[-1]
    ya = jnp.dot(ya_ref[...], wa_ref[...], preferred_element_type=F32)
    yb = jnp.dot(ob_ref[...], wb_ref[...], preferred_element_type=F32)
    yc = jnp.dot(oc_ref[...], wc_ref[...], preferred_element_type=F32)
    merged = g_ref[:, 0:D] * ya + g_ref[:, D:2 * D] * yb + g_ref[:, 2 * D:3 * D] * yc
    o_ref[...] = x_ref[...] + jnp.dot(merged.astype(BF16), wo_ref[...], preferred_element_type=F32)


def _merge(ya, ob, oc, g, x, wa, wb, wc, wo, layer, tm=512):
    T, D = x.shape
    W = ya.shape[-1]
    rows = lambda n: pl.BlockSpec((tm, n), lambda i: (i, 0))
    return pl.pallas_call(
        _merge_kernel,
        grid=(T // tm,),
        in_specs=[rows(W), rows(W), rows(W), rows(3 * D), rows(D),
                  _resident((W, D), layer), _resident((W, D), layer), _resident((W, D), layer),
                  _resident((D, D), layer)],
        out_specs=rows(D),
        out_shape=jax.ShapeDtypeStruct((T, D), F32),
        compiler_params=_cparams(("parallel",)),
        name="merge",
    )(ya, ob, oc, g, x, wa, wb, wc, wo)


FFN_TF = 2816


def _ffn_ple_kernel(x_ref, p_ref, fg_ref, wgu_ref, wd_ref, pn_ref, wpg_ref, wple_ref, fn_ref, o_ref, *, final):
    d_ff = wd_ref.shape[0]
    x = x_ref[...]
    h = _rms(x, fg_ref[...]).astype(BF16)
    for c in range(d_ff // FFN_TF):
        gate = jnp.dot(h, wgu_ref[:, c * FFN_TF:(c + 1) * FFN_TF], preferred_element_type=F32)
        up = jnp.dot(h, wgu_ref[:, d_ff + c * FFN_TF:d_ff + (c + 1) * FFN_TF], preferred_element_type=F32)
        act = (gate * _sigmoid(gate) * up).astype(BF16)
        x = x + jnp.dot(act, wd_ref[c * FFN_TF:(c + 1) * FFN_TF, :], preferred_element_type=F32)
    hn = _rms(x, pn_ref[...]).astype(BF16)
    pg = _sigmoid(jnp.dot(hn, wpg_ref[...], preferred_element_type=F32))
    pe = jnp.dot(p_ref[...].astype(BF16), wple_ref[...], preferred_element_type=F32)
    x = x + pg * pe
    if final:
        x = _rms(x, fn_ref[...])
    o_ref[...] = x


def _ffn_ple(x, p, layer, fg, w_gate_up, w_down, pn, wpg, wple, fn, final, tm=512):
    T, D = x.shape
    P = p.shape[-1]
    d_ff = w_down.shape[1]
    assert d_ff % FFN_TF == 0
    return pl.pallas_call(
        functools.partial(_ffn_ple_kernel, final=final),
        grid=(T // tm,),
        in_specs=[pl.BlockSpec((tm, D), lambda i: (i, 0)), pl.BlockSpec((None, tm, P), lambda i: (layer, i, 0)),
                  _resident((1, D)), _resident((D, 2 * d_ff), layer), _resident((d_ff, D), layer),
                  _resident((1, D)), _resident((D, D), layer), _resident((P, D), layer), _resident((1, D))],
        out_specs=pl.BlockSpec((tm, D), lambda i: (i, 0)),
        out_shape=jax.ShapeDtypeStruct((T, D), F32),
        compiler_params=_cparams(("parallel",)),
        name="ffn_ple_final" if final else "ffn_ple",
    )(x, p, fg.reshape(1, D), w_gate_up, w_down, pn.reshape(1, D), wpg, wple, fn.reshape(1, D))


def _block_diag(w):
    L, H, I, J = w.shape
    eye = jnp.eye(H, dtype=w.dtype)
    return (eye[None, :, None, :, None] * w[:, :, :, None, :]).reshape(L, H * I, H * J)


def _head_slots(w, width):
    lead = w.shape[:-1]
    w = w.reshape(*lead, MLA_HEADS, width)
    pad = [(0, 0)] * (w.ndim - 1) + [(0, HEAD_SLOT - width)]
    return jnp.pad(w, pad).reshape(*lead, MLA_HEADS * HEAD_SLOT)


def _rope_tables(S):
    pos = jnp.arange(S, dtype=F32)
    inv_freq = ROPE_BASE ** (-jnp.arange(0, MLA_ROPE, 2, dtype=F32) / MLA_ROPE)
    ang = pos[:, None] * inv_freq[None, :]
    cos, sin = jnp.cos(ang), jnp.sin(ang)
    half = MLA_ROPE // 2
    z = lambda n: jnp.zeros((S, n), F32)
    tail = LANES - ROPE_LANE0 - MLA_ROPE
    cq_t = jnp.concatenate([jnp.ones((S, ROPE_LANE0), F32), cos, cos, z(tail)], axis=1)
    ck_t = jnp.concatenate([z(ROPE_LANE0), cos, cos, z(tail)], axis=1)
    s1 = jnp.concatenate([z(ROPE_LANE0), -sin, z(half), z(tail)], axis=1)
    s2 = jnp.concatenate([z(ROPE_LANE0), z(half), sin, z(tail)], axis=1)
    return cq_t, ck_t, s1, s2


def kernel(x, p, mix_norm, w_in, gate_b, conv_w, conv_b, lru_wa, lru_ba, lru_wx, lru_bx, lru_lambda,
           mla_q_norm, mla_wuq, mla_kv_norm, mla_wukv, fox_bf, w_br_a, w_br_b, w_br_c, w_o,
           ffn_norm, w_gate_up, w_down, ple_norm, w_ple_gate, w_ple, final_norm):
    B, S, D = x.shape
    depth = w_in.shape[0]
    T = B * S
    tabs = _rope_tables(S)
    bias_tabs = _fox_bias_tables()
    mla_scale = (MLA_NOPE + MLA_ROPE) ** -0.5 * LOG2E
    fox_scale = FOX_HEAD_DIM ** -0.5 * LOG2E

    o_u = 0
    o_cq = 2 * LRU_WIDTH
    o_ckv = o_cq + MLA_Q_LORA
    o_kr = o_ckv + MLA_KV_LORA
    o_fq = o_kr + MLA_ROPE
    o_fl = o_fq + 3 * FOX_WIDTH
    o_g = o_fl + FOX_HEADS

    wb = w_in.astype(BF16)
    zcols = lambda n: jnp.zeros((depth, D, n), BF16)
    lat_w = MLA_Q_LORA + MLA_KV_LORA + LANES
    lat_pad = -lat_w % IN_TN
    nslot = FOX_HEADS * HEAD_SLOT
    w_all = jnp.concatenate([
        wb[..., o_u:o_cq],
        wb[..., o_cq:o_kr], zcols(ROPE_LANE0), wb[..., o_kr:o_fq], wb[..., o_fl:o_g],
        zcols(LANES - ROPE_LANE0 - MLA_ROPE - FOX_HEADS + lat_pad),
        _head_slots(wb[..., o_fq:o_fq + FOX_WIDTH], FOX_HEAD_DIM),
        _head_slots(wb[..., o_fq + FOX_WIDTH:o_fq + 2 * FOX_WIDTH], FOX_HEAD_DIM),
        wb[..., o_fq + 2 * FOX_WIDTH:o_fl],
        wb[..., o_g:]], axis=-1)
    widths = (2 * LRU_WIDTH, lat_w + lat_pad, 2 * nslot + FOX_WIDTH, 3 * D)
    n_before_fox, n_before_gate = widths[0] + widths[1], widths[0] + widths[1] + widths[2]
    cs = jnp.ones((1, sum(widths)), F32).at[0, n_before_fox:n_before_fox + nslot].set(fox_scale)
    cb = jnp.concatenate([jnp.zeros((depth, n_before_gate), F32), gate_b], axis=1)[:, None, :]
    wa_bd, wx_bd = _block_diag(lru_wa).astype(BF16), _block_diag(lru_wx).astype(BF16)
    wq = _head_slots(mla_wuq.astype(BF16), MLA_NOPE + MLA_ROPE)
    wukv = mla_wukv.astype(BF16).reshape(depth, MLA_KV_LORA, MLA_HEADS, MLA_NOPE + MLA_V)
    wk = _head_slots(wukv[..., :MLA_NOPE].reshape(depth, MLA_KV_LORA, MLA_HEADS * MLA_NOPE), MLA_NOPE)
    wv = wukv[..., MLA_NOPE:].reshape(depth, MLA_KV_LORA, MLA_HEADS * MLA_V)
    bf_rows = jnp.zeros((depth, 1, LANES), F32).at[:, 0, FLOGIT_LANE0:FLOGIT_LANE0 + FOX_HEADS].set(fox_bf)
    wbr_a, wbr_b, wbr_c, wo = (w.astype(BF16) for w in (w_br_a, w_br_b, w_br_c, w_o))
    wgu, wdn, wpg, wpl = (w.astype(BF16) for w in (w_gate_up, w_down, w_ple_gate, w_ple))

    xf = x.reshape(T, D)
    for i in range(depth):
        ya, lat, fox, g = _in_proj_lru(xf, mix_norm[i], w_all, cs, cb[i], widths, conv_w[i], conv_b[i], wa_bd,
                                       lru_ba[i], wx_bd, lru_bx[i], lru_lambda[i], i, B, S)
        q_m, k_m, vt_m, q_f, k_f, vt_f = _prep(lat, fox, tabs, mla_q_norm[i], mla_kv_norm[i], wq[i], wk[i], wv[i],
                                               bf_rows[i], bias_tabs, B, S, mla_scale)
        ob = _attention(q_m, k_m, vt_m, B, S, unit=MLA_CHUNK, name="mla_attention")
        oc = _attention(q_f, k_f, vt_f, B, S, unit=1, name="fox_attention")
        xf = _merge(ya.reshape(T, -1), ob.reshape(T, -1), oc.reshape(T, -1), g.reshape(T, -1), xf,
                    wbr_a, wbr_b, wbr_c, wo, i)
        xf = _ffn_ple(xf, p.reshape(depth, T, -1), i, ffn_norm[i], wgu, wdn, ple_norm[i], wpg, wpl,
                      final_norm, final=(i == depth - 1))
    return xf.reshape(B, S, D)
```

```python
import functools
import math

import jax
import jax.numpy as jnp
from jax import lax
from jax.experimental import pallas as pl
from jax.experimental.pallas import tpu as pltpu

F32 = jnp.float32
BF16 = jnp.bfloat16

EPS = 1e-6
NEG_INF = -1e30
LOG2E = math.log2(math.e)

LRU_WIDTH = 512
LRU_HEADS = 8
CONV_WIDTH = 4
LRU_C = 8.0

MLA_HEADS = 8
MLA_Q_LORA = 384
MLA_KV_LORA = 256
MLA_NOPE = 64
MLA_ROPE = 32
MLA_V = 64
ROPE_BASE = 10000.0
MLA_CHUNK = 64

FOX_HEADS = 8
FOX_HEAD_DIM = 64
FOX_WIDTH = FOX_HEADS * FOX_HEAD_DIM

LANES = 128
SUBLANES = 8
HEAD_SLOT = LANES
ROPE_LANE0 = MLA_NOPE
FLOGIT_LANE0 = MLA_NOPE + MLA_ROPE
V_ROWS = 80
Q_SUB = 256
SKEW = 4
LOOP_CHUNKS = 4

VMEM_LIMIT = 56 * 1024 * 1024


def _cparams(sem):
    return pltpu.CompilerParams(dimension_semantics=sem, vmem_limit_bytes=VMEM_LIMIT)


def _rms(xf, g):
    ms = jnp.mean(xf * xf, axis=-1, keepdims=True)
    return xf * lax.rsqrt(ms + EPS) * g


def _sigmoid(z):
    return 0.5 * jnp.tanh(0.5 * z) + 0.5


def _softplus(z):
    return jnp.maximum(z, 0.0) + jnp.log1p(jnp.exp(-jnp.abs(z)))


IN_TN = 256


def _resident(shape, layer=None):
    if layer is None:
        return pl.BlockSpec(shape, lambda *_: (0,) * len(shape), pipeline_mode=pl.Buffered(1))
    return pl.BlockSpec((None,) + tuple(shape), lambda *_: (layer,) + (0,) * len(shape), pipeline_mode=pl.Buffered(1))


def _in_proj_lru_kernel(x_ref, g_ref, w_ref, cs_ref, cb_ref, cw_ref, cvb_ref, wa_ref, ba_ref, wx_ref, bx_ref, lam_ref,
                        ya_ref, lat_ref, fox_ref, gate_ref, ext_scr, hc_scr, *, widths, tm):
    W = LRU_WIDTH
    pad = SUBLANES

    @pl.when(pl.program_id(1) == 0)
    def _():
        ext_scr[0:pad, :] = jnp.zeros((pad, W), F32)
        hc_scr[...] = jnp.zeros((1, W), F32)

    h = _rms(x_ref[...], g_ref[...]).astype(BF16)

    def tile(j):
        sl = slice(j * IN_TN, (j + 1) * IN_TN)
        return jnp.dot(h, w_ref[:, sl], preferred_element_type=F32), sl

    ntu = widths[0] // IN_TN
    uu = jnp.concatenate([tile(j)[0] for j in range(ntu)], axis=1)
    u, ug = uu[:, 0:W], uu[:, W:2 * W]
    ext_scr[pad:pad + tm, :] = u
    xc = cvb_ref[...] + ext_scr[pl.ds(pad - 3, tm), :] * cw_ref[0:1, :]
    xc = xc + ext_scr[pl.ds(pad - 2, tm), :] * cw_ref[1:2, :]
    xc = xc + ext_scr[pl.ds(pad - 1, tm), :] * cw_ref[2:3, :]
    xc = xc + u * cw_ref[3:4, :]
    ext_scr[0:pad, :] = u[tm - pad:tm, :]
    xb = xc.astype(BF16)
    r = _sigmoid(jnp.dot(xb, wa_ref[...], preferred_element_type=F32) + ba_ref[...])
    ig = _sigmoid(jnp.dot(xb, wx_ref[...], preferred_element_type=F32) + bx_ref[...])
    log_a = (-LRU_C) * r * _softplus(-lam_ref[...])
    a = jnp.exp(log_a)
    b = jnp.sqrt(jnp.tanh(-log_a) * (a * a + 1.0)) * (ig * xc)

    row = lax.broadcasted_iota(jnp.int32, (SUBLANES, W), 0)
    h_prev = hc_scr[...]
    hs = []
    for i in range(tm // SUBLANES):
        aa = a[i * SUBLANES:(i + 1) * SUBLANES, :]
        bb = b[i * SUBLANES:(i + 1) * SUBLANES, :]
        d = 1
        while d < SUBLANES:
            keep = row >= d
            a_sh = jnp.where(keep, pltpu.roll(aa, d, 0), 1.0)
            b_sh = jnp.where(keep, pltpu.roll(bb, d, 0), 0.0)
            bb = aa * b_sh + bb
            aa = aa * a_sh
            d *= 2
        hb = bb + aa * h_prev
        hs.append(hb)
        h_prev = hb[SUBLANES - 1:SUBLANES, :]
    hc_scr[...] = h_prev
    ya_ref[...] = (jnp.concatenate(hs, axis=0) * jax.nn.gelu(ug)).astype(ya_ref.dtype)

    col = widths[0]
    for slab, (o_ref, width) in enumerate(zip((lat_ref, fox_ref, gate_ref), widths[1:])):
        for j in range(width // IN_TN):
            acc, sl = tile(col // IN_TN + j)
            if slab == 1:
                acc = acc * cs_ref[:, sl]
            if slab == 2:
                acc = _sigmoid(acc + cb_ref[:, sl])
            o_ref[:, j * IN_TN:(j + 1) * IN_TN] = acc.astype(o_ref.dtype)
        col += width


def _in_proj_lru(x, g, w_all, cs, cb, widths, conv_w, conv_b, wa_bd, ba, wx_bd, bx, lam, layer, B, S, tm=512):
    D = x.shape[-1]
    N = sum(widths)
    W = LRU_WIDTH
    rowv = lambda v: v.reshape(1, W)
    out = lambda w: pl.BlockSpec((None, tm, w), lambda b, t: (b, t, 0))
    return pl.pallas_call(
        functools.partial(_in_proj_lru_kernel, widths=widths, tm=tm),
        grid=(B, S // tm),
        in_specs=[
            pl.BlockSpec((None, tm, D), lambda b, t: (b, t, 0)),
            _resident((1, D)), _resident((D, N), layer), _resident((1, N)), _resident((1, N)),
            _resident((CONV_WIDTH, W)), _resident((1, W)),
            _resident((W, W), layer), _resident((1, W)), _resident((W, W), layer), _resident((1, W)), _resident((1, W)),
        ],
        out_specs=[out(W), out(widths[1]), out(widths[2]), out(widths[3])],
        out_shape=[jax.ShapeDtypeStruct((B, S, W), BF16), jax.ShapeDtypeStruct((B, S, widths[1]), F32),
                   jax.ShapeDtypeStruct((B, S, widths[2]), BF16), jax.ShapeDtypeStruct((B, S, widths[3]), BF16)],
        scratch_shapes=[pltpu.VMEM((tm + SUBLANES, W), F32), pltpu.VMEM((1, W), F32)],
        compiler_params=_cparams(("parallel", "arbitrary")),
        name="in_proj_lru",
    )(x.reshape(B, S, D), g.reshape(1, D), w_all, cs, cb, conv_w, rowv(conv_b), wa_bd, rowv(ba), wx_bd, rowv(bx),
      rowv(lam))


def _rope_slot(x, c, s1, s2):
    return x * c + pltpu.roll(x, LANES - MLA_ROPE // 2, 1) * s1 + pltpu.roll(x, MLA_ROPE // 2, 1) * s2


def _bf16_terms(c):
    hi = c.astype(BF16).astype(F32)
    r = c - hi
    lo = r.astype(BF16).astype(F32)
    return hi, lo, (r - lo).astype(BF16).astype(F32)


def _fox_bias_tables():
    rows = jnp.arange(3 * LANES)
    term, head = rows // LANES, rows % LANES - FLOGIT_LANE0
    valid = (head >= 0) & (head < FOX_HEADS)
    cols = jnp.arange(FOX_HEADS * HEAD_SLOT)
    col_head, col_lane = cols // HEAD_SLOT, cols % HEAD_SLOT - FOX_HEAD_DIM
    hit = valid[:, None] & (head[:, None] == col_head[None, :])
    sel_q = jnp.where(hit & (col_lane[None, :] == term[:, None]), 1.0, 0.0).astype(BF16)
    sel_k = jnp.where(hit & (col_lane[None, :] == term[:, None] + 3), -1.0, 0.0).astype(BF16)
    one_q = jnp.where((col_lane >= 3) & (col_lane < 6), 1.0, 0.0).astype(F32)[None, :]
    one_k = jnp.where((col_lane >= 0) & (col_lane < 3), 1.0, 0.0).astype(F32)[None, :]
    return sel_q, sel_k, one_q, one_k


def _store_values_t(vt_ref, v, ts):
    vt = v.T
    row = lax.broadcasted_iota(jnp.int32, (V_ROWS - MLA_V, ts), 0)
    ones_tile = jnp.where(row == 0, 1.0, 0.0).astype(vt_ref.dtype)
    for h in range(MLA_HEADS):
        vt_ref[h * V_ROWS:h * V_ROWS + MLA_V, :] = vt[h * MLA_V:(h + 1) * MLA_V, :].astype(vt_ref.dtype)
        vt_ref[h * V_ROWS + MLA_V:(h + 1) * V_ROWS, :] = ones_tile


def _prep_kernel(lat_ref, fox_ref, cq_t_ref, ck_t_ref, s1_ref, s2_ref, qn_ref, kvn_ref, wq_ref, wk_ref, wv_ref,
                 bf_ref, selq_ref, selk_ref, oneq_ref, onek_ref, qm_ref, km_ref, vtm_ref, qf_ref, kf_ref, vtf_ref,
                 carry_scr, *, ts, q_scale):
    @pl.when(pl.program_id(1) == 0)
    def _():
        carry_scr[...] = jnp.zeros((1, LANES), F32)

    cq_t, ck_t, s1, s2 = cq_t_ref[...], ck_t_ref[...], s1_ref[...], s2_ref[...]

    cqn = _rms(lat_ref[:, 0:MLA_Q_LORA], qn_ref[...]).astype(BF16)
    q = jnp.dot(cqn, wq_ref[...], preferred_element_type=F32)
    for h in range(MLA_HEADS):
        sl = slice(h * HEAD_SLOT, (h + 1) * HEAD_SLOT)
        qm_ref[:, sl] = (_rope_slot(q[:, sl], cq_t, s1, s2) * q_scale).astype(qm_ref.dtype)

    ckvn = _rms(lat_ref[:, MLA_Q_LORA:MLA_Q_LORA + MLA_KV_LORA], kvn_ref[...]).astype(BF16)
    kn = jnp.dot(ckvn, wk_ref[...], preferred_element_type=F32)
    misc = lat_ref[:, MLA_Q_LORA + MLA_KV_LORA:MLA_Q_LORA + MLA_KV_LORA + LANES]
    kr = _rope_slot(misc, ck_t, s1, s2)
    for h in range(MLA_HEADS):
        sl = slice(h * HEAD_SLOT, (h + 1) * HEAD_SLOT)
        km_ref[:, sl] = (kn[:, sl] + kr).astype(km_ref.dtype)
    _store_values_t(vtm_ref, jnp.dot(ckvn, wv_ref[...], preferred_element_type=F32), ts)

    c = -_softplus(-(misc + bf_ref[...]))
    row = lax.broadcasted_iota(jnp.int32, (ts, LANES), 0)
    d = 1
    while d < ts:
        c = c + jnp.where(row >= d, pltpu.roll(c, d, 0), 0.0)
        d *= 2
    c = c + carry_scr[...]
    carry_scr[...] = c[ts - 1:ts, :]
    c = c * LOG2E

    terms = jnp.concatenate(_bf16_terms(c), axis=1).astype(BF16)
    nslot = FOX_HEADS * HEAD_SLOT
    q_bias = jnp.dot(terms, selq_ref[...], preferred_element_type=F32) + oneq_ref[...]
    k_bias = jnp.dot(terms, selk_ref[...], preferred_element_type=F32) + onek_ref[...]
    qf_ref[...] = (fox_ref[:, 0:nslot].astype(F32) + q_bias).astype(qf_ref.dtype)
    kf_ref[...] = (fox_ref[:, nslot:2 * nslot].astype(F32) + k_bias).astype(kf_ref.dtype)
    _store_values_t(vtf_ref, fox_ref[:, 2 * nslot:2 * nslot + FOX_WIDTH].astype(F32), ts)


def _prep(lat, fox, tabs, qn, kvn, wq, wk, wv, bf_row, bias_tabs, B, S, q_scale, ts=1024):
    NL, NF = lat.shape[-1], fox.shape[-1]
    full = lambda shape: pl.BlockSpec(shape, lambda b, t: (0,) * len(shape))
    tab = pl.BlockSpec((ts, LANES), lambda b, t: (t, 0))
    HS = MLA_HEADS * HEAD_SLOT
    slots = pl.BlockSpec((None, ts, HS), lambda b, t: (b, t, 0))
    vts = pl.BlockSpec((None, MLA_HEADS * V_ROWS, ts), lambda b, t: (b, 0, t))
    slot_shape = jax.ShapeDtypeStruct((B, S, HS), BF16)
    vt_shape = jax.ShapeDtypeStruct((B, MLA_HEADS * V_ROWS, S), BF16)
    return pl.pallas_call(
        functools.partial(_prep_kernel, ts=ts, q_scale=q_scale),
        grid=(B, S // ts),
        in_specs=[
            pl.BlockSpec((None, ts, NL), lambda b, t: (b, t, 0)),
            pl.BlockSpec((None, ts, NF), lambda b, t: (b, t, 0)),
            tab, tab, tab, tab,
            full((1, MLA_Q_LORA)), full((1, MLA_KV_LORA)),
            full((MLA_Q_LORA, HS)), full((MLA_KV_LORA, HS)), full((MLA_KV_LORA, MLA_HEADS * MLA_V)),
            full((1, LANES)),
            full((3 * LANES, HS)), full((3 * LANES, HS)), full((1, HS)), full((1, HS)),
        ],
        out_specs=[slots, slots, vts, slots, slots, vts],
        out_shape=[slot_shape, slot_shape, vt_shape, slot_shape, slot_shape, vt_shape],
        scratch_shapes=[pltpu.VMEM((1, LANES), F32)],
        compiler_params=_cparams(("parallel", "arbitrary")),
        name="mla_fox_prep",
    )(lat.reshape(B, S, NL), fox.reshape(B, S, NF), *tabs, qn.reshape(1, -1), kvn.reshape(1, -1), wq, wk, wv,
      bf_row, *bias_tabs)


def _attn_kernel(q_ref, k_ref, vt_ref, mask_ref, o_ref, m_scr, acc_scr, s_scr, *, tq, tk):
    qi = pl.program_id(2)
    nsub = tq // Q_SUB
    ndiag = tq // tk
    m_scr[...] = jnp.full(m_scr.shape, NEG_INF, F32)
    acc_scr[...] = jnp.zeros(acc_scr.shape, F32)

    def diag_kinds(d):
        kinds = []
        for c in range(nsub):
            if (c + 1) * Q_SUB <= d * tk:
                kinds.append(None)
            elif (d + 1) * tk <= c * Q_SUB:
                kinds.append((tk, None))
            else:
                kinds.append((min(tk, (c + 1) * Q_SUB - d * tk), d * tk))
        return kinds

    def units(k0, kinds, slot0):
        live = [(h, c) for c in range(nsub) for h in range(2) if kinds[c] is not None]
        return [(k0, kinds[c][0], kinds[c][1], h, c, slot0 + idx) for idx, (h, c) in enumerate(live)]

    def logits_one(k0, klen, h, c, slot):
        kh = k_ref[pl.ds(k0, klen), h * HEAD_SLOT:(h + 1) * HEAD_SLOT]
        qh = q_ref[c * Q_SUB:(c + 1) * Q_SUB, h * HEAD_SLOT:(h + 1) * HEAD_SLOT]
        s_scr[slot, 0:klen] = lax.dot_general(kh, qh, (((1,), (1,)), ((), ())), preferred_element_type=F32)

    def softmax_one(k0, klen, mask_off, h, c, slot):
        st = s_scr[slot, 0:klen]
        cs = slice(c * Q_SUB, (c + 1) * Q_SUB)
        if mask_off is not None:
            st = st + mask_ref[(c * Q_SUB - mask_off) // Q_SUB, 0:klen, :]
        m_prev = m_scr[h, :, cs]
        m_new = jnp.maximum(m_prev, jnp.max(st, axis=0, keepdims=True))
        alpha = jnp.exp2(m_prev - m_new)
        pt = jnp.exp2(st - m_new).astype(BF16)
        m_scr[h, :, cs] = m_new
        vth = vt_ref[h * V_ROWS:(h + 1) * V_ROWS, pl.ds(k0, klen)]
        acc_scr[h, :, cs] = alpha * acc_scr[h, :, cs] + jnp.dot(vth, pt, preferred_element_type=F32)

    def run(todo, ahead):
        for j in range(len(todo)):
            if j < len(ahead):
                k0, klen, _, h, c, slot = ahead[j]
                logits_one(k0, klen, h, c, slot)
            softmax_one(*todo[j])

    n = qi * ndiag
    per = 2 * nsub
    all_visible = [(tk, None)] * nsub
    for k0, klen, _, h, c, slot in units(0, all_visible, 0)[:SKEW]:
        logits_one(k0, klen, h, c, slot)

    def body(i, carry):
        chunks = [units(pl.multiple_of((LOOP_CHUNKS * i + t) * tk, tk), all_visible, (t % 2) * per)
                  for t in range(LOOP_CHUNKS + 1)]
        flat = sum(chunks, [])
        run(flat[:LOOP_CHUNKS * per], flat[SKEW:SKEW + LOOP_CHUNKS * per])
        return carry

    lax.fori_loop(0, n // LOOP_CHUNKS, body, 0)
    flat = []
    for d in range(ndiag):
        flat += units(pl.multiple_of(qi * tq + d * tk, tk), diag_kinds(d), (d % 2) * per)
    run(flat, flat[SKEW:])

    outs = []
    for h in range(2):
        a = acc_scr[h]
        outs.append(a[0:MLA_V, :] / a[MLA_V:MLA_V + 1, :])
    o_ref[...] = jnp.concatenate(outs, axis=0).T.astype(o_ref.dtype)


def _causal_bias(unit, tk):
    key = jnp.arange(tk)[None, :, None] // unit
    qry = (jnp.arange(Q_SUB)[None, None, :] + Q_SUB * jnp.arange(tk // Q_SUB)[:, None, None]) // unit
    return jnp.where(qry >= key, 0.0, NEG_INF).astype(F32)


def _attention(q, k, vt, B, S, *, unit, name, tq=2048, tk=512):
    npair = MLA_HEADS // 2
    nstream = 2 * (tq // Q_SUB)
    assert (tq // tk) % LOOP_CHUNKS == 0 and LOOP_CHUNKS % 2 == 0
    return pl.pallas_call(
        functools.partial(_attn_kernel, tq=tq, tk=tk),
        grid=(B, npair, S // tq),
        in_specs=[
            pl.BlockSpec((None, tq, 2 * HEAD_SLOT), lambda b, p, i: (b, i, p)),
            pl.BlockSpec((None, S, 2 * HEAD_SLOT), lambda b, p, i: (b, 0, p)),
            pl.BlockSpec((None, 2 * V_ROWS, S), lambda b, p, i: (b, p, 0)),
            pl.BlockSpec((tk // Q_SUB, tk, Q_SUB), lambda b, p, i: (0, 0, 0)),
        ],
        out_specs=pl.BlockSpec((None, tq, 2 * MLA_V), lambda b, p, i: (b, i, p)),
        out_shape=jax.ShapeDtypeStruct((B, S, npair * 2 * MLA_V), BF16),
        scratch_shapes=[pltpu.VMEM((2, 1, tq), F32), pltpu.VMEM((2, V_ROWS, tq), F32),
                        pltpu.VMEM((2 * nstream, tk, Q_SUB), F32)],
        compiler_params=_cparams(("parallel", "parallel", "arbitrary")),
        name=name,
    )(q, k, vt, _causal_bias(unit, tk))


def _merge_kernel(ya_ref, ob_ref, oc_ref, g_ref, x_ref, wa_ref, wb_ref, wc_ref, wo_ref, o_ref):
    D = x_ref.shape[-1]
    ya = jnp.dot(ya_ref[...], wa_ref[...], preferred_element_type=F32)
    yb = jnp.dot(ob_ref[...], wb_ref[...], preferred_element_type=F32)
    yc = jnp.dot(oc_ref[...], wc_ref[...], preferred_element_type=F32)
    merged = g_ref[:, 0:D] * ya + g_ref[:, D:2 * D] * yb + g_ref[:, 2 * D:3 * D] * yc
    o_ref[...] = x_ref[...] + jnp.dot(merged.astype(BF16), wo_ref[...], preferred_element_type=F32)


def _merge(ya, ob, oc, g, x, wa, wb, wc, wo, layer, tm=1024):
    T, D = x.shape
    W = ya.shape[-1]
    rows = lambda n: pl.BlockSpec((tm, n), lambda i: (i, 0))
    return pl.pallas_call(
        _merge_kernel,
        grid=(T // tm,),
        in_specs=[rows(W), rows(W), rows(W), rows(3 * D), rows(D),
                  _resident((W, D), layer), _resident((W, D), layer), _resident((W, D), layer),
                  _resident((D, D), layer)],
        out_specs=rows(D),
        out_shape=jax.ShapeDtypeStruct((T, D), F32),
        compiler_params=_cparams(("parallel",)),
        name="merge",
    )(ya, ob, oc, g, x, wa, wb, wc, wo)


FFN_TF = 2816


def _ffn_ple_kernel(x_ref, p_ref, fg_ref, wgu_ref, wd_ref, pn_ref, wpg_ref, wple_ref, fn_ref, o_ref, *, final):
    d_ff = wd_ref.shape[0]
    x = x_ref[...]
    h = _rms(x, fg_ref[...]).astype(BF16)
    for c in range(d_ff // FFN_TF):
        gate = jnp.dot(h, wgu_ref[:, c * FFN_TF:(c + 1) * FFN_TF], preferred_element_type=F32)
        up = jnp.dot(h, wgu_ref[:, d_ff + c * FFN_TF:d_ff + (c + 1) * FFN_TF], preferred_element_type=F32)
        act = (gate * _sigmoid(gate) * up).astype(BF16)
        x = x + jnp.dot(act, wd_ref[c * FFN_TF:(c + 1) * FFN_TF, :], preferred_element_type=F32)
    hn = _rms(x, pn_ref[...]).astype(BF16)
    pg = _sigmoid(jnp.dot(hn, wpg_ref[...], preferred_element_type=F32))
    pe = jnp.dot(p_ref[...].astype(BF16), wple_ref[...], preferred_element_type=F32)
    x = x + pg * pe
    if final:
        x = _rms(x, fn_ref[...])
    o_ref[...] = x


def _ffn_ple(x, p, layer, fg, w_gate_up, w_down, pn, wpg, wple, fn, final, tm=1024):
    T, D = x.shape
    P = p.shape[-1]
    d_ff = w_down.shape[1]
    assert d_ff % FFN_TF == 0
    return pl.pallas_call(
        functools.partial(_ffn_ple_kernel, final=final),
        grid=(T // tm,),
        in_specs=[pl.BlockSpec((tm, D), lambda i: (i, 0)), pl.BlockSpec((None, tm, P), lambda i: (layer, i, 0)),
                  _resident((1, D)), _resident((D, 2 * d_ff), layer), _resident((d_ff, D), layer),
                  _resident((1, D)), _resident((D, D), layer), _resident((P, D), layer), _resident((1, D))],
        out_specs=pl.BlockSpec((tm, D), lambda i: (i, 0)),
        out_shape=jax.ShapeDtypeStruct((T, D), F32),
        compiler_params=_cparams(("parallel",)),
        name="ffn_ple_final" if final else "ffn_ple",
    )(x, p, fg.reshape(1, D), w_gate_up, w_down, pn.reshape(1, D), wpg, wple, fn.reshape(1, D))


def _block_diag(w):
    L, H, I, J = w.shape
    eye = jnp.eye(H, dtype=w.dtype)
    return (eye[None, :, None, :, None] * w[:, :, :, None, :]).reshape(L, H * I, H * J)


def _head_slots(w, width):
    lead = w.shape[:-1]
    w = w.reshape(*lead, MLA_HEADS, width)
    pad = [(0, 0)] * (w.ndim - 1) + [(0, HEAD_SLOT - width)]
    return jnp.pad(w, pad).reshape(*lead, MLA_HEADS * HEAD_SLOT)


def _rope_tables(S):
    pos = jnp.arange(S, dtype=F32)
    inv_freq = ROPE_BASE ** (-jnp.arange(0, MLA_ROPE, 2, dtype=F32) / MLA_ROPE)
    ang = pos[:, None] * inv_freq[None, :]
    cos, sin = jnp.cos(ang), jnp.sin(ang)
    half = MLA_ROPE // 2
    z = lambda n: jnp.zeros((S, n), F32)
    tail = LANES - ROPE_LANE0 - MLA_ROPE
    cq_t = jnp.concatenate([jnp.ones((S, ROPE_LANE0), F32), cos, cos, z(tail)], axis=1)
    ck_t = jnp.concatenate([z(ROPE_LANE0), cos, cos, z(tail)], axis=1)
    s1 = jnp.concatenate([z(ROPE_LANE0), -sin, z(half), z(tail)], axis=1)
    s2 = jnp.concatenate([z(ROPE_LANE0), z(half), sin, z(tail)], axis=1)
    return cq_t, ck_t, s1, s2


def kernel(x, p, mix_norm, w_in, gate_b, conv_w, conv_b, lru_wa, lru_ba, lru_wx, lru_bx, lru_lambda,
           mla_q_norm, mla_wuq, mla_kv_norm, mla_wukv, fox_bf, w_br_a, w_br_b, w_br_c, w_o,
           ffn_norm, w_gate_up, w_down, ple_norm, w_ple_gate, w_ple, final_norm):
    B, S, D = x.shape
    depth = w_in.shape[0]
    T = B * S
    tabs = _rope_tables(S)
    bias_tabs = _fox_bias_tables()
    mla_scale = (MLA_NOPE + MLA_ROPE) ** -0.5 * LOG2E
    fox_scale = FOX_HEAD_DIM ** -0.5 * LOG2E

    o_u = 0
    o_cq = 2 * LRU_WIDTH
    o_ckv = o_cq + MLA_Q_LORA
    o_kr = o_ckv + MLA_KV_LORA
    o_fq = o_kr + MLA_ROPE
    o_fl = o_fq + 3 * FOX_WIDTH
    o_g = o_fl + FOX_HEADS

    wb = w_in.astype(BF16)
    zcols = lambda n: jnp.zeros((depth, D, n), BF16)
    lat_w = MLA_Q_LORA + MLA_KV_LORA + LANES
    lat_pad = -lat_w % IN_TN
    nslot = FOX_HEADS * HEAD_SLOT
    w_all = jnp.concatenate([
        wb[..., o_u:o_cq],
        wb[..., o_cq:o_kr], zcols(ROPE_LANE0), wb[..., o_kr:o_fq], wb[..., o_fl:o_g],
        zcols(LANES - ROPE_LANE0 - MLA_ROPE - FOX_HEADS + lat_pad),
        _head_slots(wb[..., o_fq:o_fq + FOX_WIDTH], FOX_HEAD_DIM),
        _head_slots(wb[..., o_fq + FOX_WIDTH:o_fq + 2 * FOX_WIDTH], FOX_HEAD_DIM),
        wb[..., o_fq + 2 * FOX_WIDTH:o_fl],
        wb[..., o_g:]], axis=-1)
    widths = (2 * LRU_WIDTH, lat_w + lat_pad, 2 * nslot + FOX_WIDTH, 3 * D)
    n_before_fox, n_before_gate = widths[0] + widths[1], widths[0] + widths[1] + widths[2]
    cs = jnp.ones((1, sum(widths)), F32).at[0, n_before_fox:n_before_fox + nslot].set(fox_scale)
    cb = jnp.concatenate([jnp.zeros((depth, n_before_gate), F32), gate_b], axis=1)[:, None, :]
    wa_bd, wx_bd = _block_diag(lru_wa).astype(BF16), _block_diag(lru_wx).astype(BF16)
    wq = _head_slots(mla_wuq.astype(BF16), MLA_NOPE + MLA_ROPE)
    wukv = mla_wukv.astype(BF16).reshape(depth, MLA_KV_LORA, MLA_HEADS, MLA_NOPE + MLA_V)
    wk = _head_slots(wukv[..., :MLA_NOPE].reshape(depth, MLA_KV_LORA, MLA_HEADS * MLA_NOPE), MLA_NOPE)
    wv = wukv[..., MLA_NOPE:].reshape(depth, MLA_KV_LORA, MLA_HEADS * MLA_V)
    bf_rows = jnp.zeros((depth, 1, LANES), F32).at[:, 0, FLOGIT_LANE0:FLOGIT_LANE0 + FOX_HEADS].set(fox_bf)
    wbr_a, wbr_b, wbr_c, wo = (w.astype(BF16) for w in (w_br_a, w_br_b, w_br_c, w_o))
    wgu, wdn, wpg, wpl = (w.astype(BF16) for w in (w_gate_up, w_down, w_ple_gate, w_ple))

    xf = x.reshape(T, D)
    for i in range(depth):
        ya, lat, fox, g = _in_proj_lru(xf, mix_norm[i], w_all, cs, cb[i], widths, conv_w[i], conv_b[i], wa_bd,
                                       lru_ba[i], wx_bd, lru_bx[i], lru_lambda[i], i, B, S)
        q_m, k_m, vt_m, q_f, k_f, vt_f = _prep(lat, fox, tabs, mla_q_norm[i], mla_kv_norm[i], wq[i], wk[i], wv[i],
                                               bf_rows[i], bias_tabs, B, S, mla_scale)
        ob = _attention(q_m, k_m, vt_m, B, S, unit=MLA_CHUNK, name="mla_attention")
        oc = _attention(q_f, k_f, vt_f, B, S, unit=1, name="fox_attention")
        xf = _merge(ya.reshape(T, -1), ob.reshape(T, -1), oc.reshape(T, -1), g.reshape(T, -1), xf,
                    wbr_a, wbr_b, wbr_c, wo, i)
        xf = _ffn_ple(xf, p.reshape(depth, T, -1), i, ffn_norm[i], wgu, wdn, ple_norm[i], wpg, wpl,
                      final_norm, final=(i == depth - 1))
    return xf.reshape(B, S, D)
```
